```python
import jax, jax.numpy as jnp
from jax import lax
import numpy as np

D_MODEL = 4096
BATCH = 2
SEQ = 4096
DEPTH = 2

CHUNK = 64
Q_BLOCK = 128
HEAD_DIM = 128
N_HEADS = D_MODEL // HEAD_DIM
N_HEADS_A = N_HEADS // 2
N_HEADS_B = N_HEADS - N_HEADS_A
N_HEADS_C = N_HEADS
WIDTH_A = N_HEADS_A * HEAD_DIM
WIDTH_B = N_HEADS_B * HEAD_DIM
WIDTH_C = N_HEADS_C * HEAD_DIM
LEFT_CHUNKS = 8
BAND_CHUNKS = LEFT_CHUNKS + 1
REL_CLIP = 128
N_REL = 2 * REL_CLIP + 1
RMS_EPS = 1e-6
N_EVEN = (DEPTH + 1) // 2
N_ODD = DEPTH // 2
IN_EVEN = 4 * WIDTH_A + 4 * WIDTH_B + N_HEADS_A
IN_ODD = 4 * WIDTH_C

kernel_name = "hybrid_fox_chunkrel_stickbreak_sandwich"


def rms_norm(x, g):
    xf = x.astype(jnp.float32)
    y = xf * lax.rsqrt(jnp.mean(xf * xf, axis=-1, keepdims=True) + RMS_EPS)
    return (y * g.astype(jnp.float32)).astype(x.dtype)


def split_cols(proj, widths):
    outs, off = [], 0
    for w in widths:
        outs.append(proj[..., off:off + w])
        off += w
    return outs


def heads(t, n_heads):
    b, s, _ = t.shape
    return t.reshape(b, s, n_heads, HEAD_DIM)


def forgetting_attention(q, k, v, log_f):
    b, s_len, h, dh = q.shape
    scale = dh ** -0.5
    c = jnp.transpose(jnp.cumsum(log_f, axis=1), (0, 2, 1))
    outs = []
    for i in range(s_len // Q_BLOCK):
        q0, q1 = i * Q_BLOCK, (i + 1) * Q_BLOCK
        logits = jnp.einsum('bqhd,bkhd->bhqk', q[:, q0:q1], k[:, :q1],
                            preferred_element_type=jnp.float32) * scale
        decay = c[:, :, q0:q1, None] - c[:, :, None, :q1]
        tq = jnp.arange(q0, q1)[:, None]
        tk = jnp.arange(q1)[None, :]
        logits = jnp.where(tk <= tq, logits + decay, -jnp.inf)
        p = jax.nn.softmax(logits, axis=-1).astype(v.dtype)
        outs.append(jnp.einsum('bhqk,bkhd->bqhd', p, v[:, :q1]))
    return jnp.concatenate(outs, axis=1)


def chunked_relpos_attention(q, k, v, rel_bias):
    b, s_len, h, dh = q.shape
    scale = dh ** -0.5
    nc = s_len // CHUNK
    band = BAND_CHUNKS * CHUNK
    qc = q.reshape(b, nc, CHUNK, h, dh)
    pad = ((0, 0), (LEFT_CHUNKS, 0), (0, 0), (0, 0), (0, 0))
    kc = jnp.pad(k.reshape(b, nc, CHUNK, h, dh), pad)
    vc = jnp.pad(v.reshape(b, nc, CHUNK, h, dh), pad)
    logits = jnp.concatenate(
        [jnp.einsum('bnqhd,bnkhd->bhnqk', qc, kc[:, j:j + nc],
                    preferred_element_type=jnp.float32) for j in range(BAND_CHUNKS)],
        axis=-1) * scale
    qi = jnp.arange(CHUNK)[:, None]
    km = jnp.arange(band)[None, :]
    rel = jnp.clip(LEFT_CHUNKS * CHUNK + qi - km, -REL_CLIP, REL_CLIP) + REL_CLIP
    bias = rel_bias.astype(jnp.float32)[:, rel]
    src_chunk = jnp.arange(nc)[:, None] - LEFT_CHUNKS + km // CHUNK
    valid = (src_chunk >= 0)[None, None, :, None, :]
    logits = jnp.where(valid, logits + bias[None, :, None], -jnp.inf)
    p = jax.nn.softmax(logits, axis=-1).astype(v.dtype)
    out = jnp.einsum('bhnqk,bnkhd->bnqhd', p[..., :CHUNK], vc[:, 0:nc])
    for j in range(1, BAND_CHUNKS):
        out = out + jnp.einsum('bhnqk,bnkhd->bnqhd',
                               p[..., j * CHUNK:(j + 1) * CHUNK], vc[:, j:j + nc])
    return out.reshape(b, s_len, h, dh)


def stick_breaking_attention(q, k, v):
    b, s_len, h, dh = q.shape
    scale = dh ** -0.5
    outs = []
    for i in range(s_len // Q_BLOCK):
        q0, q1 = i * Q_BLOCK, (i + 1) * Q_BLOCK
        z = jnp.einsum('bqhd,bkhd->bhqk', q[:, q0:q1], k[:, :q1],
                       preferred_element_type=jnp.float32) * scale
        tq = jnp.arange(q0, q1)[:, None]
        tk = jnp.arange(q1)[None, :]
        causal = tk < tq
        log_beta = jax.nn.log_sigmoid(z)
        log_one_minus = jnp.where(causal, jax.nn.log_sigmoid(-z), 0.0)
        tail = lax.cumsum(log_one_minus, axis=3, reverse=True) - log_one_minus
        a = jnp.where(causal, jnp.exp(log_beta + tail), 0.0).astype(v.dtype)
        outs.append(jnp.einsum('bhqk,bkhd->bqhd', a, v[:, :q1]))
    return jnp.concatenate(outs, axis=1)


def even_mixer(h, w_in, b_f, rel_bias, w_out):
    b, s_len, _ = h.shape
    proj = jnp.einsum('bsd,de->bse', h, w_in)
    aq, ak, av, ag, bq, bk, bv, bg, af = split_cols(
        proj, [WIDTH_A] * 4 + [WIDTH_B] * 4 + [N_HEADS_A])
    log_f = jax.nn.log_sigmoid((af + b_f).astype(jnp.float32))
    oa = forgetting_attention(heads(aq, N_HEADS_A), heads(ak, N_HEADS_A),
                              heads(av, N_HEADS_A), log_f).reshape(b, s_len, WIDTH_A)
    ob = chunked_relpos_attention(heads(bq, N_HEADS_B), heads(bk, N_HEADS_B),
                                  heads(bv, N_HEADS_B), rel_bias).reshape(b, s_len, WIDTH_B)
    mixed = jnp.concatenate([oa * jax.nn.silu(ag), ob * jax.nn.silu(bg)], axis=-1)
    return jnp.einsum('bse,ed->bsd', mixed, w_out)


def odd_mixer(h, w_in, w_out):
    b, s_len, _ = h.shape
    proj = jnp.einsum('bsd,de->bse', h, w_in)
    cq, ck, cv, cg = split_cols(proj, [WIDTH_C] * 4)
    oc = stick_breaking_attention(heads(cq, N_HEADS_C), heads(ck, N_HEADS_C),
                                  heads(cv, N_HEADS_C)).reshape(b, s_len, WIDTH_C)
    return jnp.einsum('bse,ed->bsd', oc * jax.nn.silu(cg), w_out)


def setup_inputs(seed: int = 0) -> dict:
    key = jax.random.key(seed)
    ks = jax.random.split(key, 10)
    fan = D_MODEL ** -0.5
    x = jax.random.normal(ks[0], (BATCH, SEQ, D_MODEL), jnp.float32)
    norm_pre = 1.0 + 0.05 * jax.random.normal(ks[1], (DEPTH, D_MODEL), jnp.float32)
    norm_post = 1.0 + 0.05 * jax.random.normal(ks[2], (DEPTH, D_MODEL), jnp.float32)
    w_in_even = jax.random.normal(ks[3], (N_EVEN, D_MODEL, IN_EVEN), jnp.float32) * fan
    b_f_even = jax.random.uniform(ks[4], (N_EVEN, N_HEADS_A), jnp.float32, 1.0, 4.0)
    rel_bias_even = 0.1 * jax.random.normal(ks[5], (N_EVEN, N_HEADS_B, N_REL), jnp.float32)
    w_out_even = jax.random.normal(ks[6], (N_EVEN, WIDTH_A + WIDTH_B, D_MODEL), jnp.float32) * (WIDTH_A + WIDTH_B) ** -0.5
    w_in_odd = jax.random.normal(ks[7], (N_ODD, D_MODEL, IN_ODD), jnp.float32) * fan
    w_out_odd = jax.random.normal(ks[8], (N_ODD, WIDTH_C, D_MODEL), jnp.float32) * WIDTH_C ** -0.5
    return {"x": x, "norm_pre": norm_pre, "norm_post": norm_post,
            "w_in_even": w_in_even, "b_f_even": b_f_even,
            "rel_bias_even": rel_bias_even, "w_out_even": w_out_even,
            "w_in_odd": w_in_odd, "w_out_odd": w_out_odd}


def reference(x, norm_pre, norm_post, w_in_even, b_f_even, rel_bias_even, w_out_even, w_in_odd, w_out_odd):
    for layer in range(DEPTH):
        h = rms_norm(x, norm_pre[layer])
        if layer % 2 == 0:
            e = layer // 2
            y = even_mixer(h, w_in_even[e], b_f_even[e], rel_bias_even[e], w_out_even[e])
        else:
            o = layer // 2
            y = odd_mixer(h, w_in_odd[o], w_out_odd[o])
        x = x + rms_norm(y, norm_post[layer])
    return x
```

```python
import functools

import jax
import jax.numpy as jnp
from jax import lax
from jax.experimental import pallas as pl
from jax.experimental.pallas import tpu as pltpu

HEAD_DIM = 128
CHUNK = 64
LEFT_CHUNKS = 8
REL_CLIP = 128
N_REL = 2 * REL_CLIP + 1
RMS_EPS = 1e-6

LANES = 128
VMEM_LIMIT = 56 * 1024 * 1024
NEG = -1e30

BF16 = jnp.bfloat16
F32 = jnp.float32


def _params(n_axes):
    return pltpu.CompilerParams(dimension_semantics=("arbitrary",) * n_axes,
                                vmem_limit_bytes=VMEM_LIMIT)


def _dot_nt(a, b):
    return lax.dot_general(a, b, (((1,), (1,)), ((), ())), preferred_element_type=F32)


def _dot(a, b):
    return jnp.dot(a, b, preferred_element_type=F32)


def _split3(x):
    hi = x.astype(BF16)
    r1 = x - hi.astype(F32)
    mid = r1.astype(BF16)
    lo = (r1 - mid.astype(F32)).astype(BF16)
    return hi, mid, lo


def _log_sigmoid(z):
    return jnp.minimum(z, 0.0) - jnp.log1p(jnp.exp(-jnp.abs(z)))


def _silu(g):
    return g / (1.0 + jnp.exp(-g))


def _rmsnorm_kernel(x_ref, g_ref, o_ref):
    x = x_ref[...]
    ms = jnp.mean(x * x, axis=-1, keepdims=True)
    o_ref[...] = (x * lax.rsqrt(ms + RMS_EPS) * g_ref[...]).astype(o_ref.dtype)


def _rmsnorm(x, g, out_dtype, bm=256):
    m, d = x.shape
    return pl.pallas_call(
        _rmsnorm_kernel,
        grid=(m // bm,),
        in_specs=[pl.BlockSpec((bm, d), lambda i: (i, 0)),
                  pl.BlockSpec((1, d), lambda i: (0, 0))],
        out_specs=pl.BlockSpec((bm, d), lambda i: (i, 0)),
        out_shape=jax.ShapeDtypeStruct((m, d), out_dtype),
        compiler_params=_params(1),
        name="rmsnorm",
    )(x, g.reshape(1, d))


def _postnorm_kernel(x_ref, y_ref, g_ref, o_ref):
    y = y_ref[...]
    ms = jnp.mean(y * y, axis=-1, keepdims=True)
    o_ref[...] = x_ref[...] + y * lax.rsqrt(ms + RMS_EPS) * g_ref[...]


def _postnorm_residual(x, y, g, bm=256):
    m, d = x.shape
    return pl.pallas_call(
        _postnorm_kernel,
        grid=(m // bm,),
        in_specs=[pl.BlockSpec((bm, d), lambda i: (i, 0)),
                  pl.BlockSpec((bm, d), lambda i: (i, 0)),
                  pl.BlockSpec((1, d), lambda i: (0, 0))],
        out_specs=pl.BlockSpec((bm, d), lambda i: (i, 0)),
        out_shape=jax.ShapeDtypeStruct((m, d), F32),
        compiler_params=_params(1),
        name="postnorm_residual",
    )(x, y, g.reshape(1, d))


def _matmul_kernel(x_ref, w_ref, o_ref):
    o_ref[...] = _dot(x_ref[...], w_ref[...]).astype(o_ref.dtype)


def _matmul(x, w, out_dtype, *, bm, bn, col_block0=0, n_col_blocks=None, name="matmul"):
    m, k = x.shape
    if n_col_blocks is None:
        n_col_blocks = w.shape[1] // bn
    return pl.pallas_call(
        _matmul_kernel,
        grid=(m // bm, n_col_blocks),
        in_specs=[pl.BlockSpec((bm, k), lambda i, j: (i, 0)),
                  pl.BlockSpec((k, bn), lambda i, j: (0, j + col_block0))],
        out_specs=pl.BlockSpec((bm, bn), lambda i, j: (i, j)),
        out_shape=jax.ShapeDtypeStruct((m, n_col_blocks * bn), out_dtype),
        compiler_params=_params(2),
        name=name,
    )(x, w)


def _matmul2_kernel(xa_ref, xb_ref, wa_ref, wb_ref, o_ref):
    o_ref[...] = (_dot(xa_ref[...], wa_ref[...]) + _dot(xb_ref[...], wb_ref[...])).astype(o_ref.dtype)


def _matmul_concat_k(xa, xb, w, out_dtype, *, bm, bn):
    m, ka = xa.shape
    kb = xb.shape[1]
    assert ka == kb and w.shape[0] == ka + kb
    n = w.shape[1]
    return pl.pallas_call(
        _matmul2_kernel,
        grid=(m // bm, n // bn),
        in_specs=[pl.BlockSpec((bm, ka), lambda i, j: (i, 0)),
                  pl.BlockSpec((bm, kb), lambda i, j: (i, 0)),
                  pl.BlockSpec((ka, bn), lambda i, j: (0, j)),
                  pl.BlockSpec((kb, bn), lambda i, j: (1, j))],
        out_specs=pl.BlockSpec((bm, bn), lambda i, j: (i, j)),
        out_shape=jax.ShapeDtypeStruct((m, n), out_dtype),
        compiler_params=_params(2),
        name="out_proj_even",
    )(xa, xb, w, w)


def _forget_cumsum_kernel(af_ref, aft_ref, bcol_ref, brow_ref, ccol_ref, crow_ref, *, tc):
    s_len, nh = af_ref.shape
    r = lax.broadcasted_iota(jnp.int32, (tc, tc), 0)
    c = lax.broadcasted_iota(jnp.int32, (tc, tc), 1)
    lower = (c <= r).astype(F32).astype(BF16)
    upper = (r <= c).astype(F32).astype(BF16)

    def body(i, carry):
        carry_row, carry_col = carry
        t0 = pl.multiple_of(i * tc, tc)
        lf = _log_sigmoid(af_ref[pl.ds(t0, tc), :] + brow_ref[...])
        lft = _log_sigmoid(aft_ref[:, pl.ds(t0, tc)] + bcol_ref[...])
        a, b, d = _split3(lf)
        cc = (_dot(lower, a) + _dot(lower, b) + _dot(lower, d)) + carry_row
        a, b, d = _split3(lft)
        cr = (_dot(a, upper) + _dot(b, upper) + _dot(d, upper)) + carry_col
        ccol_ref[pl.ds(t0, tc), :] = cc
        crow_ref[:, pl.ds(t0, tc)] = cr
        return cc[tc - 1:tc, :], cr[:, tc - 1:tc]

    lax.fori_loop(0, s_len // tc, body,
                  (jnp.zeros((1, nh), F32), jnp.zeros((nh, 1), F32)))


def _forget_cumsum(af, b_f, tc=256):
    bsz, s_len, nh = af.shape
    aft = jnp.transpose(af, (0, 2, 1))
    return pl.pallas_call(
        functools.partial(_forget_cumsum_kernel, tc=tc),
        grid=(bsz,),
        in_specs=[pl.BlockSpec((None, s_len, nh), lambda b: (b, 0, 0)),
                  pl.BlockSpec((None, nh, s_len), lambda b: (b, 0, 0)),
                  pl.BlockSpec((nh, 1), lambda b: (0, 0)),
                  pl.BlockSpec((1, nh), lambda b: (0, 0))],
        out_specs=[pl.BlockSpec((None, s_len, nh), lambda b: (b, 0, 0)),
                   pl.BlockSpec((None, nh, s_len), lambda b: (b, 0, 0))],
        out_shape=[jax.ShapeDtypeStruct((bsz, s_len, nh), F32),
                   jax.ShapeDtypeStruct((bsz, nh, s_len), F32)],
        compiler_params=_params(1),
        name="forget_cumsum",
    )(af, aft, b_f.reshape(nh, 1), b_f.reshape(1, nh))


def _fox_kernel(q_ref, k_ref, v_ref, g_ref, crow_ref, ccol_ref, o_ref, *, tq, scale):
    h = pl.program_id(1)
    s_len = q_ref.shape[0]
    lane = lax.broadcasted_iota(jnp.int32, (tq, ccol_ref.shape[-1]), 1)
    row = lax.broadcasted_iota(jnp.int32, (tq, tq), 0)
    col = lax.broadcasted_iota(jnp.int32, (tq, tq), 1)
    causal = col <= row

    def q_body(qi, _):
        q0 = pl.multiple_of(qi * tq, tq)
        q = q_ref[pl.ds(q0, tq), :]
        c_t = jnp.sum(jnp.where(lane == h, ccol_ref[pl.ds(q0, tq), :], 0.0), axis=1, keepdims=True)

        def tile(k0, carry, masked):
            m, l, acc = carry
            k = k_ref[pl.ds(k0, tq), :]
            v = v_ref[pl.ds(k0, tq), :]
            s = _dot_nt(q, k) * scale + (c_t - crow_ref[:, pl.ds(k0, tq)])
            if masked:
                s = jnp.where(causal, s, NEG)
            m_new = jnp.maximum(m, jnp.max(s, axis=1, keepdims=True))
            alpha = jnp.exp(m - m_new)
            p = jnp.exp(s - m_new)
            l = alpha * l + jnp.sum(p, axis=1, keepdims=True)
            acc = alpha * acc + _dot(p.astype(v.dtype), v)
            return m_new, l, acc

        init = (jnp.full((tq, 1), NEG, F32), jnp.zeros((tq, 1), F32), jnp.zeros((tq, HEAD_DIM), F32))
        carry = lax.fori_loop(
            0, qi, lambda kj, cr: tile(pl.multiple_of(kj * tq, tq), cr, False), init)
        _, l, acc = tile(q0, carry, True)
        g = g_ref[pl.ds(q0, tq), :].astype(F32)
        o_ref[pl.ds(q0, tq), :] = ((acc / l) * _silu(g)).astype(o_ref.dtype)
        return 0

    lax.fori_loop(0, s_len // tq, q_body, 0)


def _fox_attention(proj, c_row, c_col, bsz, s_len, n_heads, col_blocks, tq=256):
    qb, kb, vb, gb = col_blocks
    scale = HEAD_DIM ** -0.5

    def head_spec(off):
        return pl.BlockSpec((s_len, HEAD_DIM), lambda b, h: (b, off + h))

    return pl.pallas_call(
        functools.partial(_fox_kernel, tq=tq, scale=scale),
        grid=(bsz, n_heads),
        in_specs=[head_spec(qb), head_spec(kb), head_spec(vb), head_spec(gb),
                  pl.BlockSpec((None, None, 1, s_len), lambda b, h: (b, h, 0, 0)),
                  pl.BlockSpec((None, s_len, n_heads), lambda b, h: (b, 0, 0))],
        out_specs=pl.BlockSpec((s_len, HEAD_DIM), lambda b, h: (b, h)),
        out_shape=jax.ShapeDtypeStruct((bsz * s_len, n_heads * HEAD_DIM), BF16),
        compiler_params=_params(2),
        name="fox_attention",
    )(proj, proj, proj, proj, c_row.reshape(bsz, n_heads, 1, s_len), c_col)


def _chunk_kernel(q_ref, k_ref, v_ref, g_ref, rel_ref, o_ref, kpad, vpad, bias_ref, *, tq, scale):
    b = pl.program_id(1)
    s_len = q_ref.shape[0]
    pad = LEFT_CHUNKS * CHUNK
    win = tq + pad
    rp = rel_ref.shape[-1]

    @pl.when(b == 0)
    def _build_bias():
        u = lax.broadcasted_iota(jnp.int32, (rp, win), 1)
        r = lax.broadcasted_iota(jnp.int32, (rp, win), 0)
        idx = jnp.where(u < pad + CHUNK, jnp.clip(pad - u, -REL_CLIP, REL_CLIP) + REL_CLIP, 2 * REL_CLIP)
        sel = (r == idx).astype(F32).astype(BF16)
        a, bb, d = _split3(jnp.broadcast_to(rel_ref[...], (8, rp)))
        g = _dot(a, sel) + _dot(bb, sel) + _dot(d, sel)
        tile = pltpu.roll(jnp.broadcast_to(g[0:1, :], (tq, win)), 0, 1, stride=1, stride_axis=0)
        qc = lax.broadcasted_iota(jnp.int32, (tq, win), 0) // CHUNK
        kc = lax.broadcasted_iota(jnp.int32, (tq, win), 1) // CHUNK
        bias_ref[...] = jnp.where((kc >= qc) & (kc <= qc + LEFT_CHUNKS), tile, NEG)

    kpad[pl.ds(0, pad), :] = jnp.zeros((pad, HEAD_DIM), kpad.dtype)
    vpad[pl.ds(0, pad), :] = jnp.zeros((pad, HEAD_DIM), vpad.dtype)
    kpad[pl.ds(pad, s_len), :] = k_ref[...]
    vpad[pl.ds(pad, s_len), :] = v_ref[...]
    key_pos = lax.broadcasted_iota(jnp.int32, (tq, win), 1)

    def body(j, _):
        q0 = pl.multiple_of(j * tq, tq)
        q = q_ref[pl.ds(q0, tq), :]
        s = _dot_nt(q, kpad[pl.ds(q0, win), :]) * scale + bias_ref[...]
        s = jnp.where(key_pos + q0 >= pad, s, NEG)
        p = jnp.exp(s - jnp.max(s, axis=1, keepdims=True))
        l = jnp.sum(p, axis=1, keepdims=True)
        o = _dot(p.astype(vpad.dtype), vpad[pl.ds(q0, win), :]) / l
        g = g_ref[pl.ds(q0, tq), :].astype(F32)
        o_ref[pl.ds(q0, tq), :] = (o * _silu(g)).astype(o_ref.dtype)
        return 0

    lax.fori_loop(0, s_len // tq, body, 0)


def _chunk_attention(proj, rel_bias, bsz, s_len, n_heads, col_blocks, tq=256):
    qb, kb, vb, gb = col_blocks
    scale = HEAD_DIM ** -0.5
    pad = LEFT_CHUNKS * CHUNK
    rp = 3 * LANES
    assert N_REL <= rp
    rel = jnp.pad(rel_bias, ((0, 0), (0, rp - N_REL))).reshape(n_heads, 1, rp)

    def head_spec(off):
        return pl.BlockSpec((s_len, HEAD_DIM), lambda h, b: (b, off + h))

    return pl.pallas_call(
        functools.partial(_chunk_kernel, tq=tq, scale=scale),
        grid=(n_heads, bsz),
        in_specs=[head_spec(qb), head_spec(kb), head_spec(vb), head_spec(gb),
                  pl.BlockSpec((None, 1, rp), lambda h, b: (h, 0, 0))],
        out_specs=pl.BlockSpec((s_len, HEAD_DIM), lambda h, b: (b, h)),
        out_shape=jax.ShapeDtypeStruct((bsz * s_len, n_heads * HEAD_DIM), BF16),
        scratch_shapes=[pltpu.VMEM((s_len + pad, HEAD_DIM), BF16),
                        pltpu.VMEM((s_len + pad, HEAD_DIM), BF16),
                        pltpu.VMEM((tq, tq + pad), F32)],
        compiler_params=_params(2),
        name="chunk_attention",
    )(proj, proj, proj, proj, rel)


def _sb_kernel(q_ref, k_ref, v_ref, g_ref, o_ref, *, tq, scale):
    s_len = q_ref.shape[0]
    row = lax.broadcasted_iota(jnp.int32, (tq, tq), 0)
    col = lax.broadcasted_iota(jnp.int32, (tq, tq), 1)
    strict = col < row
    after = (row > col).astype(F32).astype(BF16)

    def q_body(qi, _):
        q0 = pl.multiple_of(qi * tq, tq)
        q = q_ref[pl.ds(q0, tq), :]

        def tile(k0, carry, masked):
            rsum, acc = carry
            k = k_ref[pl.ds(k0, tq), :]
            v = v_ref[pl.ds(k0, tq), :]
            z = _dot_nt(q, k) * scale
            log_beta = _log_sigmoid(z)
            log_om = log_beta - z
            if masked:
                log_om = jnp.where(strict, log_om, 0.0)
            hi = log_om.astype(BF16)
            lo = (log_om - hi.astype(F32)).astype(BF16)
            tail = _dot(hi, after) + _dot(lo, after)
            a = jnp.exp(log_beta + tail + rsum)
            if masked:
                a = jnp.where(strict, a, 0.0)
            acc = acc + _dot(a.astype(v.dtype), v)
            rsum = rsum + (tail[:, 0:1] + log_om[:, 0:1])
            return rsum, acc

        carry = tile(q0, (jnp.zeros((tq, 1), F32), jnp.zeros((tq, HEAD_DIM), F32)), True)
        _, acc = lax.fori_loop(
            0, qi, lambda it, cr: tile(pl.multiple_of((qi - 1 - it) * tq, tq), cr, False), carry)
        g = g_ref[pl.ds(q0, tq), :].astype(F32)
        o_ref[pl.ds(q0, tq), :] = (acc * _silu(g)).astype(o_ref.dtype)
        return 0

    lax.fori_loop(0, s_len // tq, q_body, 0)


def _sb_attention(proj, bsz, s_len, n_heads, col_blocks, tq=256):
    qb, kb, vb, gb = col_blocks
    scale = HEAD_DIM ** -0.5

    def head_spec(off):
        return pl.BlockSpec((s_len, HEAD_DIM), lambda b, h: (b, off + h))

    return pl.pallas_call(
        functools.partial(_sb_kernel, tq=tq, scale=scale),
        grid=(bsz, n_heads),
        in_specs=[head_spec(qb), head_spec(kb), head_spec(vb), head_spec(gb)],
        out_specs=pl.BlockSpec((s_len, HEAD_DIM), lambda b, h: (b, h)),
        out_shape=jax.ShapeDtypeStruct((bsz * s_len, n_heads * HEAD_DIM), BF16),
        compiler_params=_params(2),
        name="sb_attention",
    )(proj, proj, proj, proj)


def _even_layer(x2, bsz, s_len, g_pre, g_post, w_in, b_f, rel_bias, w_out):
    d = x2.shape[1]
    n_a = b_f.shape[0]
    n_b = rel_bias.shape[0]
    h = _rmsnorm(x2, g_pre, BF16)
    w_in16 = w_in.astype(BF16)
    main_cols = 4 * (n_a + n_b) * HEAD_DIM
    bn = 1024
    proj = _matmul(h, w_in16, BF16, bm=1024, bn=bn, n_col_blocks=main_cols // bn, name="in_proj_even")
    af = _matmul(h, w_in16, F32, bm=1024, bn=LANES, col_block0=main_cols // LANES, n_col_blocks=1,
                 name="forget_proj")[:, :n_a]
    c_col, c_row = _forget_cumsum(af.reshape(bsz, s_len, n_a), b_f)
    mix_a = _fox_attention(proj, c_row, c_col, bsz, s_len, n_a,
                           (0, n_a, 2 * n_a, 3 * n_a))
    off = 4 * n_a
    mix_b = _chunk_attention(proj, rel_bias, bsz, s_len, n_b,
                             (off, off + n_b, off + 2 * n_b, off + 3 * n_b))
    y = _matmul_concat_k(mix_a, mix_b, w_out.astype(BF16), F32, bm=1024, bn=1024)
    return _postnorm_residual(x2, y, g_post)


def _odd_layer(x2, bsz, s_len, g_pre, g_post, w_in, w_out):
    n_c = w_out.shape[0] // HEAD_DIM
    h = _rmsnorm(x2, g_pre, BF16)
    proj = _matmul(h, w_in.astype(BF16), BF16, bm=1024, bn=1024, name="in_proj_odd")
    mix = _sb_attention(proj, bsz, s_len, n_c, (0, n_c, 2 * n_c, 3 * n_c))
    y = _matmul(mix, w_out.astype(BF16), F32, bm=1024, bn=1024, name="out_proj_odd")
    return _postnorm_residual(x2, y, g_post)


def kernel(x, norm_pre, norm_post, w_in_even, b_f_even, rel_bias_even, w_out_even, w_in_odd, w_out_odd):
    bsz, s_len, d = x.shape
    depth = norm_pre.shape[0]
    x2 = x.reshape(bsz * s_len, d)
    for layer in range(depth):
        i = layer // 2
        if layer % 2 == 0:
            x2 = _even_layer(x2, bsz, s_len, norm_pre[layer], norm_post[layer],
                             w_in_even[i], b_f_even[i], rel_bias_even[i], w_out_even[i])
        else:
            x2 = _odd_layer(x2, bsz, s_len, norm_pre[layer], norm_post[layer],
                            w_in_odd[i], w_out_odd[i])
    return x2.reshape(bsz, s_len, d)
```

```python
import functools

import jax
import jax.numpy as jnp
from jax import lax
from jax.experimental import pallas as pl
from jax.experimental.pallas import tpu as pltpu

HEAD_DIM = 128
CHUNK = 64
LEFT_CHUNKS = 8
REL_CLIP = 128
N_REL = 2 * REL_CLIP + 1
RMS_EPS = 1e-6

LANES = 128
VMEM_LIMIT = 56 * 1024 * 1024
PROJ_BM, PROJ_BN = 512, 1024
NEG = -1e30
LOG2E = 1.4426950408889634
SB_STOP_LOG2 = -152.0

BF16 = jnp.bfloat16
F32 = jnp.float32


def _params(n_axes):
    return pltpu.CompilerParams(dimension_semantics=("arbitrary",) * n_axes,
                                vmem_limit_bytes=VMEM_LIMIT)


def _dot_nt(a, b):
    return lax.dot_general(a, b, (((1,), (1,)), ((), ())), preferred_element_type=F32)


def _dot(a, b):
    return jnp.dot(a, b, preferred_element_type=F32)


def _split3(x):
    hi = x.astype(BF16)
    r1 = x - hi.astype(F32)
    mid = r1.astype(BF16)
    lo = (r1 - mid.astype(F32)).astype(BF16)
    return hi, mid, lo


def _log_sigmoid(z):
    return jnp.minimum(z, 0.0) - jnp.log1p(jnp.exp(-jnp.abs(z)))


def _silu(g):
    return g / (1.0 + jnp.exp(-g))


def _rmsnorm_kernel(x_ref, g_ref, o_ref):
    x = x_ref[...]
    ms = jnp.mean(x * x, axis=-1, keepdims=True)
    o_ref[...] = (x * lax.rsqrt(ms + RMS_EPS) * g_ref[...]).astype(o_ref.dtype)


def _rmsnorm(x, g, out_dtype, bm=256):
    m, d = x.shape
    return pl.pallas_call(
        _rmsnorm_kernel,
        grid=(m // bm,),
        in_specs=[pl.BlockSpec((bm, d), lambda i: (i, 0)),
                  pl.BlockSpec((1, d), lambda i: (0, 0))],
        out_specs=pl.BlockSpec((bm, d), lambda i: (i, 0)),
        out_shape=jax.ShapeDtypeStruct((m, d), out_dtype),
        compiler_params=_params(1),
        name="rmsnorm",
    )(x, g.reshape(1, d))


def _postnorm_kernel(x_ref, y_ref, g_ref, o_ref):
    y = y_ref[...]
    ms = jnp.mean(y * y, axis=-1, keepdims=True)
    o_ref[...] = x_ref[...] + y * lax.rsqrt(ms + RMS_EPS) * g_ref[...]


def _postnorm_residual(x, y, g, bm=256):
    m, d = x.shape
    return pl.pallas_call(
        _postnorm_kernel,
        grid=(m // bm,),
        in_specs=[pl.BlockSpec((bm, d), lambda i: (i, 0)),
                  pl.BlockSpec((bm, d), lambda i: (i, 0)),
                  pl.BlockSpec((1, d), lambda i: (0, 0))],
        out_specs=pl.BlockSpec((bm, d), lambda i: (i, 0)),
        out_shape=jax.ShapeDtypeStruct((m, d), F32),
        compiler_params=_params(1),
        name="postnorm_residual",
    )(x, y, g.reshape(1, d))


def _matmul_kernel(*refs, n_parts):
    x_refs, w_refs = refs[:n_parts], refs[n_parts:2 * n_parts]
    o_ref = refs[2 * n_parts]
    w16_refs = refs[2 * n_parts + 1:]

    @pl.when(pl.program_id(1) == 0)
    def _round_weights():
        for w_ref, w16_ref in zip(w_refs, w16_refs):
            w16_ref[...] = w_ref[...].astype(BF16)

    acc = _dot(x_refs[0][...], w16_refs[0][...])
    for x_ref, w16_ref in zip(x_refs[1:], w16_refs[1:]):
        acc = acc + _dot(x_ref[...], w16_ref[...])
    o_ref[...] = acc.astype(o_ref.dtype)


def _matmul(xs, w, out_dtype, *, bm, bn, col_block0=0, n_col_blocks=None, name="matmul"):
    m = xs[0].shape[0]
    kp = xs[0].shape[1]
    n_parts = len(xs)
    assert all(x.shape == (m, kp) for x in xs) and w.shape[0] == n_parts * kp
    if n_col_blocks is None:
        n_col_blocks = w.shape[1] // bn

    def w_spec(p):
        return pl.BlockSpec((kp, bn), lambda j, i: (p, j + col_block0))

    return pl.pallas_call(
        functools.partial(_matmul_kernel, n_parts=n_parts),
        grid=(n_col_blocks, m // bm),
        in_specs=[pl.BlockSpec((bm, kp), lambda j, i: (i, 0)) for _ in xs]
                 + [w_spec(p) for p in range(n_parts)],
        out_specs=pl.BlockSpec((bm, bn), lambda j, i: (i, j)),
        out_shape=jax.ShapeDtypeStruct((m, n_col_blocks * bn), out_dtype),
        scratch_shapes=[pltpu.VMEM((kp, bn), BF16) for _ in xs],
        compiler_params=_params(2),
        name=name,
    )(*xs, *([w] * n_parts))


def _forget_cumsum_kernel(af_ref, aft_ref, bcol_ref, brow_ref, ccol_ref, crow_ref, *, tc):
    s_len, nh = af_ref.shape
    r = lax.broadcasted_iota(jnp.int32, (tc, tc), 0)
    c = lax.broadcasted_iota(jnp.int32, (tc, tc), 1)
    lower = (c <= r).astype(F32).astype(BF16)
    upper = (r <= c).astype(F32).astype(BF16)

    def body(i, carry):
        carry_row, carry_col = carry
        t0 = pl.multiple_of(i * tc, tc)
        lf = _log_sigmoid(af_ref[pl.ds(t0, tc), :] + brow_ref[...])
        lft = _log_sigmoid(aft_ref[:, pl.ds(t0, tc)] + bcol_ref[...])
        a, b, d = _split3(lf)
        cc = (_dot(lower, a) + _dot(lower, b) + _dot(lower, d)) + carry_row
        a, b, d = _split3(lft)
        cr = (_dot(a, upper) + _dot(b, upper) + _dot(d, upper)) + carry_col
        ccol_ref[pl.ds(t0, tc), :] = cc * LOG2E
        crow_ref[:, pl.ds(t0, tc)] = cr * LOG2E
        return cc[tc - 1:tc, :], cr[:, tc - 1:tc]

    lax.fori_loop(0, s_len // tc, body,
                  (jnp.zeros((1, nh), F32), jnp.zeros((nh, 1), F32)))


def _forget_cumsum(af, b_f, tc=256):
    bsz, s_len, nh = af.shape
    aft = jnp.transpose(af, (0, 2, 1))
    return pl.pallas_call(
        functools.partial(_forget_cumsum_kernel, tc=tc),
        grid=(bsz,),
        in_specs=[pl.BlockSpec((None, s_len, nh), lambda b: (b, 0, 0)),
                  pl.BlockSpec((None, nh, s_len), lambda b: (b, 0, 0)),
                  pl.BlockSpec((nh, 1), lambda b: (0, 0)),
                  pl.BlockSpec((1, nh), lambda b: (0, 0))],
        out_specs=[pl.BlockSpec((None, s_len, nh), lambda b: (b, 0, 0)),
                   pl.BlockSpec((None, nh, s_len), lambda b: (b, 0, 0))],
        out_shape=[jax.ShapeDtypeStruct((bsz, s_len, nh), F32),
                   jax.ShapeDtypeStruct((bsz, nh, s_len), F32)],
        compiler_params=_params(1),
        name="forget_cumsum",
    )(af, aft, b_f.reshape(nh, 1), b_f.reshape(1, nh))


def _fox_kernel(q_ref, k_ref, v_ref, g_ref, crow_ref, ccol_ref, o_ref, *, tq, tk, scale2, hp):
    h0 = pl.program_id(1) * hp
    s_len = q_ref.shape[0]
    lane = lax.broadcasted_iota(jnp.int32, (tq, ccol_ref.shape[-1]), 1)
    row = lax.broadcasted_iota(jnp.int32, (tq, tk), 0)
    col = lax.broadcasted_iota(jnp.int32, (tq, tk), 1)
    heads = [slice(i * HEAD_DIM, (i + 1) * HEAD_DIM) for i in range(hp)]

    def q_body(qi, _):
        q0 = pl.multiple_of(qi * tq, tq)
        qs = [q_ref[pl.ds(q0, tq), hs] for hs in heads]
        c_blk = ccol_ref[pl.ds(q0, tq), :]
        c_ts = [jnp.sum(jnp.where(lane == h0 + i, c_blk, 0.0), axis=1, keepdims=True) for i in range(hp)]
        n_full = q0 // tk

        def logits(k0):
            return tuple(_dot_nt(qs[i], k_ref[pl.ds(k0, tk), hs]) * scale2 - crow_ref[i, :, pl.ds(k0, tk)]
                         for i, hs in enumerate(heads))

        def absorb(us, k0, state, masked):
            out = []
            for i, hs in enumerate(heads):
                m, l, acc = state[i]
                u = us[i]
                if masked:
                    u = jnp.where(col + k0 <= row + q0, u, NEG)
                m_new = jnp.maximum(m, jnp.max(u, axis=1, keepdims=True) + c_ts[i])
                alpha = jnp.exp2(m - m_new)
                p = jnp.exp2(u - (m_new - c_ts[i]))
                l = alpha * l + jnp.sum(p, axis=1, keepdims=True)
                acc = alpha * acc + _dot(p.astype(BF16), v_ref[pl.ds(k0, tk), hs])
                out.append((m_new, l, acc))
            return tuple(out)

        def step(kj, state):
            k0 = pl.multiple_of(kj * tk, tk)
            return absorb(logits(k0), k0, state, False)

        state = tuple((jnp.full((tq, 1), NEG, F32), jnp.zeros((tq, 1), F32), jnp.zeros((tq, HEAD_DIM), F32))
                      for _ in heads)
        state = lax.fori_loop(0, n_full, step, state)
        k_diag = pl.multiple_of(n_full * tk, tk)
        state = absorb(logits(k_diag), k_diag, state, True)
        for i, hs in enumerate(heads):
            _, l, acc = state[i]
            g = g_ref[pl.ds(q0, tq), hs].astype(F32)
            o_ref[pl.ds(q0, tq), hs] = ((acc / l) * _silu(g)).astype(o_ref.dtype)
        return 0

    lax.fori_loop(0, s_len // tq, q_body, 0)


def _fox_attention(proj, c_row, c_col, bsz, s_len, n_heads, col_blocks, tq=256, tk=1024, hp=2):
    qb, kb, vb, gb = col_blocks
    scale2 = LOG2E * HEAD_DIM ** -0.5
    assert n_heads % hp == 0 and all(off % hp == 0 for off in col_blocks)
    assert tk % tq == 0 and s_len % tk == 0

    def head_spec(off):
        return pl.BlockSpec((s_len, hp * HEAD_DIM), lambda b, h: (b, off // hp + h))

    return pl.pallas_call(
        functools.partial(_fox_kernel, tq=tq, tk=tk, scale2=scale2, hp=hp),
        grid=(bsz, n_heads // hp),
        in_specs=[head_spec(qb), head_spec(kb), head_spec(vb), head_spec(gb),
                  pl.BlockSpec((None, hp, 1, s_len), lambda b, h: (b, h, 0, 0)),
                  pl.BlockSpec((None, s_len, n_heads), lambda b, h: (b, 0, 0))],
        out_specs=pl.BlockSpec((s_len, hp * HEAD_DIM), lambda b, h: (b, h)),
        out_shape=jax.ShapeDtypeStruct((bsz * s_len, n_heads * HEAD_DIM), BF16),
        compiler_params=_params(2),
        name="fox_attention",
    )(proj, proj, proj, proj, c_row.reshape(bsz, n_heads, 1, s_len), c_col)


def _chunk_kernel(q_ref, k_ref, v_ref, g_ref, rel_ref, o_ref, kpad, vpad, bias_ref, *, tq, scale):
    b = pl.program_id(1)
    s_len = q_ref.shape[0]
    pad = LEFT_CHUNKS * CHUNK
    win = tq + pad
    rp = rel_ref.shape[-1]

    @pl.when(b == 0)
    def _build_bias():
        u = lax.broadcasted_iota(jnp.int32, (rp, win), 1)
        r = lax.broadcasted_iota(jnp.int32, (rp, win), 0)
        idx = jnp.where(u < pad + CHUNK, jnp.clip(pad - u, -REL_CLIP, REL_CLIP) + REL_CLIP, 2 * REL_CLIP)
        sel = (r == idx).astype(F32).astype(BF16)
        a, bb, d = _split3(jnp.broadcast_to(rel_ref[...], (8, rp)))
        g = _dot(a, sel) + _dot(bb, sel) + _dot(d, sel)
        tile = pltpu.roll(jnp.broadcast_to(g[0:1, :], (tq, win)), 0, 1, stride=1, stride_axis=0)
        qc = lax.broadcasted_iota(jnp.int32, (tq, win), 0) // CHUNK
        kc = lax.broadcasted_iota(jnp.int32, (tq, win), 1) // CHUNK
        bias_ref[...] = jnp.where((kc >= qc) & (kc <= qc + LEFT_CHUNKS), tile, NEG)

    kpad[pl.ds(0, pad), :] = jnp.zeros((pad, HEAD_DIM), kpad.dtype)
    vpad[pl.ds(0, pad), :] = jnp.zeros((pad, HEAD_DIM), vpad.dtype)
    kpad[pl.ds(pad, s_len), :] = k_ref[...]
    vpad[pl.ds(pad, s_len), :] = v_ref[...]
    key_pos = lax.broadcasted_iota(jnp.int32, (tq, win), 1)

    def body(j, _):
        q0 = pl.multiple_of(j * tq, tq)
        q = q_ref[pl.ds(q0, tq), :]
        s = _dot_nt(q, kpad[pl.ds(q0, win), :]) * scale + bias_ref[...]
        s = jnp.where(key_pos + q0 >= pad, s, NEG)
        p = jnp.exp(s - jnp.max(s, axis=1, keepdims=True))
        l = jnp.sum(p, axis=1, keepdims=True)
        o = _dot(p.astype(vpad.dtype), vpad[pl.ds(q0, win), :]) / l
        g = g_ref[pl.ds(q0, tq), :].astype(F32)
        o_ref[pl.ds(q0, tq), :] = (o * _silu(g)).astype(o_ref.dtype)
        return 0

    lax.fori_loop(0, s_len // tq, body, 0)


def _chunk_attention(proj, rel_bias, bsz, s_len, n_heads, col_blocks, tq=256):
    qb, kb, vb, gb = col_blocks
    scale = HEAD_DIM ** -0.5
    pad = LEFT_CHUNKS * CHUNK
    rp = 3 * LANES
    assert N_REL <= rp
    rel = jnp.pad(rel_bias, ((0, 0), (0, rp - N_REL))).reshape(n_heads, 1, rp)

    def head_spec(off):
        return pl.BlockSpec((s_len, HEAD_DIM), lambda h, b: (b, off + h))

    return pl.pallas_call(
        functools.partial(_chunk_kernel, tq=tq, scale=scale),
        grid=(n_heads, bsz),
        in_specs=[head_spec(qb), head_spec(kb), head_spec(vb), head_spec(gb),
                  pl.BlockSpec((None, 1, rp), lambda h, b: (h, 0, 0))],
        out_specs=pl.BlockSpec((s_len, HEAD_DIM), lambda h, b: (b, h)),
        out_shape=jax.ShapeDtypeStruct((bsz * s_len, n_heads * HEAD_DIM), BF16),
        scratch_shapes=[pltpu.VMEM((s_len + pad, HEAD_DIM), BF16),
                        pltpu.VMEM((s_len + pad, HEAD_DIM), BF16),
                        pltpu.VMEM((tq, tq + pad), F32)],
        compiler_params=_params(2),
        name="chunk_attention",
    )(proj, proj, proj, proj, rel)


def _sb_kernel(q_ref, k_ref, v_ref, g_ref, o_ref, *, tq, scale2, hp):
    s_len = q_ref.shape[0]
    row = lax.broadcasted_iota(jnp.int32, (tq, tq), 0)
    col = lax.broadcasted_iota(jnp.int32, (tq, tq), 1)
    strict = col < row
    after = (row > col).astype(F32).astype(BF16)
    after2 = jnp.concatenate([after, after], axis=0)
    heads = [slice(i * HEAD_DIM, (i + 1) * HEAD_DIM) for i in range(hp)]

    def log2_terms(q, k):
        z = _dot_nt(q, k) * scale2
        lb = jnp.minimum(z, 0.0) - jnp.log2(1.0 + jnp.exp2(-jnp.abs(z)))
        return lb, lb - z

    def suffix_sums(lom):
        hi = lom.astype(BF16)
        lo = (lom - hi.astype(F32)).astype(BF16)
        tail = _dot(jnp.concatenate([hi, lo], axis=1), after2)
        return tail, tail[:, 0:1] + lom[:, 0:1]

    def finish(q0, hs, acc):
        g = g_ref[pl.ds(q0, tq), hs].astype(F32)
        o_ref[pl.ds(q0, tq), hs] = (acc * _silu(g)).astype(o_ref.dtype)

    def diag_tile(q, q0, hs):
        lb, lom = log2_terms(q, k_ref[pl.ds(q0, tq), hs])
        lom = jnp.where(strict, lom, 0.0)
        tail, rsum = suffix_sums(lom)
        a = jnp.where(strict, jnp.exp2(lb + tail), 0.0)
        return a, rsum

    for hs in heads:
        a, _ = diag_tile(q_ref[pl.ds(0, tq), hs], 0, hs)
        finish(0, hs, _dot(a.astype(BF16), v_ref[pl.ds(0, tq), hs]))

    def more(carry):
        j, rsum, _ = carry
        return jnp.logical_and(j >= 0, jnp.max(rsum) >= SB_STOP_LOG2)

    def q_body(qi, _):
        q0 = pl.multiple_of(qi * tq, tq)
        k1 = pl.multiple_of(q0 - tq, tq)
        first = []
        for hs in heads:
            q = q_ref[pl.ds(q0, tq), hs]
            a_d, rsum_d = diag_tile(q, q0, hs)
            lb, lom = log2_terms(q, k_ref[pl.ds(k1, tq), hs])
            tail, rsum_l = suffix_sums(lom)
            a_l = jnp.exp2(lb + tail + rsum_d)
            a = jnp.concatenate([a_l, a_d], axis=1).astype(BF16)
            first.append((q, rsum_d + rsum_l, _dot(a, v_ref[pl.ds(k1, 2 * tq), hs])))

        for hs, (q, rsum, acc) in zip(heads, first):
            def tile(carry, q=q, hs=hs):
                j, rsum, acc = carry
                k0 = pl.multiple_of(j * tq, tq)
                lb, lom = log2_terms(q, k_ref[pl.ds(k0, tq), hs])
                tail, rs = suffix_sums(lom)
                a = jnp.exp2(lb + tail + rsum)
                acc = acc + _dot(a.astype(BF16), v_ref[pl.ds(k0, tq), hs])
                return j - 1, rsum + rs, acc

            _, _, acc = lax.while_loop(more, tile, (qi - 2, rsum, acc))
            finish(q0, hs, acc)
        return 0

    lax.fori_loop(1, s_len // tq, q_body, 0)


def _sb_attention(proj, bsz, s_len, n_heads, col_blocks, tq=256, hp=2):
    qb, kb, vb, gb = col_blocks
    scale2 = LOG2E * HEAD_DIM ** -0.5
    assert n_heads % hp == 0 and all(off % hp == 0 for off in col_blocks)

    def head_spec(off):
        return pl.BlockSpec((s_len, hp * HEAD_DIM), lambda b, h: (b, off // hp + h))

    return pl.pallas_call(
        functools.partial(_sb_kernel, tq=tq, scale2=scale2, hp=hp),
        grid=(bsz, n_heads // hp),
        in_specs=[head_spec(qb), head_spec(kb), head_spec(vb), head_spec(gb)],
        out_specs=pl.BlockSpec((s_len, hp * HEAD_DIM), lambda b, h: (b, h)),
        out_shape=jax.ShapeDtypeStruct((bsz * s_len, n_heads * HEAD_DIM), BF16),
        compiler_params=_params(2),
        name="sb_attention",
    )(proj, proj, proj, proj)


def _even_layer(x2, bsz, s_len, g_pre, g_post, w_in, b_f, rel_bias, w_out):
    d = x2.shape[1]
    n_a = b_f.shape[0]
    n_b = rel_bias.shape[0]
    h = _rmsnorm(x2, g_pre, BF16)
    main_cols = 4 * (n_a + n_b) * HEAD_DIM
    proj = _matmul([h], w_in, BF16, bm=PROJ_BM, bn=PROJ_BN, n_col_blocks=main_cols // PROJ_BN,
                   name="in_proj_even")
    af = _matmul([h], w_in, F32, bm=1024, bn=LANES, col_block0=main_cols // LANES, n_col_blocks=1,
                 name="forget_proj")[:, :n_a]
    c_col, c_row = _forget_cumsum(af.reshape(bsz, s_len, n_a), b_f)
    mix_a = _fox_attention(proj, c_row, c_col, bsz, s_len, n_a,
                           (0, n_a, 2 * n_a, 3 * n_a))
    off = 4 * n_a
    mix_b = _chunk_attention(proj, rel_bias, bsz, s_len, n_b,
                             (off, off + n_b, off + 2 * n_b, off + 3 * n_b))
    y = _matmul([mix_a, mix_b], w_out, F32, bm=PROJ_BM, bn=PROJ_BN, name="out_proj_even")
    return _postnorm_residual(x2, y, g_post)


def _odd_layer(x2, bsz, s_len, g_pre, g_post, w_in, w_out):
    n_c = w_out.shape[0] // HEAD_DIM
    h = _rmsnorm(x2, g_pre, BF16)
    proj = _matmul([h], w_in, BF16, bm=PROJ_BM, bn=PROJ_BN, name="in_proj_odd")
    mix = _sb_attention(proj, bsz, s_len, n_c, (0, n_c, 2 * n_c, 3 * n_c))
    y = _matmul([mix], w_out, F32, bm=PROJ_BM, bn=PROJ_BN, name="out_proj_odd")
    return _postnorm_residual(x2, y, g_post)


def kernel(x, norm_pre, norm_post, w_in_even, b_f_even, rel_bias_even, w_out_even, w_in_odd, w_out_odd):
    bsz, s_len, d = x.shape
    depth = norm_pre.shape[0]
    x2 = x.reshape(bsz * s_len, d)
    for layer in range(depth):
        i = layer // 2
        if layer % 2 == 0:
            x2 = _even_layer(x2, bsz, s_len, norm_pre[layer], norm_post[layer],
                             w_in_even[i], b_f_even[i], rel_bias_even[i], w_out_even[i])
        else:
            x2 = _odd_layer(x2, bsz, s_len, norm_pre[layer], norm_post[layer],
                            w_in_odd[i], w_out_odd[i])
    return x2.reshape(bsz, s_len, d)
```

```python
import functools

import jax
import jax.numpy as jnp
from jax import lax
from jax.experimental import pallas as pl
from jax.experimental.pallas import tpu as pltpu

HEAD_DIM = 128
CHUNK = 64
LEFT_CHUNKS = 8
REL_CLIP = 128
N_REL = 2 * REL_CLIP + 1
RMS_EPS = 1e-6

LANES = 128
VMEM_LIMIT = 56 * 1024 * 1024
PROJ_BM, PROJ_BN = 512, 1024
NEG = -1e30
LOG2E = 1.4426950408889634
SB_STOP_LOG2 = -152.0

BF16 = jnp.bfloat16
F32 = jnp.float32


def _params(n_axes):
    return pltpu.CompilerParams(dimension_semantics=("arbitrary",) * n_axes,
                                vmem_limit_bytes=VMEM_LIMIT)


def _dot_nt(a, b):
    return lax.dot_general(a, b, (((1,), (1,)), ((), ())), preferred_element_type=F32)


def _dot(a, b):
    return jnp.dot(a, b, preferred_element_type=F32)


def _split3(x):
    hi = x.astype(BF16)
    r1 = x - hi.astype(F32)
    mid = r1.astype(BF16)
    lo = (r1 - mid.astype(F32)).astype(BF16)
    return hi, mid, lo


def _log_sigmoid(z):
    return jnp.minimum(z, 0.0) - jnp.log1p(jnp.exp(-jnp.abs(z)))


def _silu(g):
    return g / (1.0 + jnp.exp(-g))


def _rmsnorm_kernel(x_ref, g_ref, o_ref):
    x = x_ref[...]
    ms = jnp.mean(x * x, axis=-1, keepdims=True)
    o_ref[...] = (x * lax.rsqrt(ms + RMS_EPS) * g_ref[...]).astype(o_ref.dtype)


def _rmsnorm(x, g, out_dtype, bm=256):
    m, d = x.shape
    return pl.pallas_call(
        _rmsnorm_kernel,
        grid=(m // bm,),
        in_specs=[pl.BlockSpec((bm, d), lambda i: (i, 0)),
                  pl.BlockSpec((1, d), lambda i: (0, 0))],
        out_specs=pl.BlockSpec((bm, d), lambda i: (i, 0)),
        out_shape=jax.ShapeDtypeStruct((m, d), out_dtype),
        compiler_params=_params(1),
        name="rmsnorm",
    )(x, g.reshape(1, d))


def _postnorm_kernel(x_ref, y_ref, g_ref, o_ref):
    y = y_ref[...]
    ms = jnp.mean(y * y, axis=-1, keepdims=True)
    o_ref[...] = x_ref[...] + y * lax.rsqrt(ms + RMS_EPS) * g_ref[...]


def _postnorm_residual(x, y, g, bm=256):
    m, d = x.shape
    return pl.pallas_call(
        _postnorm_kernel,
        grid=(m // bm,),
        in_specs=[pl.BlockSpec((bm, d), lambda i: (i, 0)),
                  pl.BlockSpec((bm, d), lambda i: (i, 0)),
                  pl.BlockSpec((1, d), lambda i: (0, 0))],
        out_specs=pl.BlockSpec((bm, d), lambda i: (i, 0)),
        out_shape=jax.ShapeDtypeStruct((m, d), F32),
        compiler_params=_params(1),
        name="postnorm_residual",
    )(x, y, g.reshape(1, d))


def _matmul_kernel(*refs, n_parts, w_transposed):
    x_refs, w_refs = refs[:n_parts], refs[n_parts:2 * n_parts]
    o_ref = refs[2 * n_parts]
    w16_refs = refs[2 * n_parts + 1:]
    dot = _dot_nt if w_transposed else _dot

    @pl.when(pl.program_id(1) == 0)
    def _round_weights():
        for w_ref, w16_ref in zip(w_refs, w16_refs):
            w16_ref[...] = w_ref[...].astype(BF16)

    acc = dot(x_refs[0][...], w16_refs[0][...])
    for x_ref, w16_ref in zip(x_refs[1:], w16_refs[1:]):
        acc = acc + dot(x_ref[...], w16_ref[...])
    o_ref[...] = acc.astype(o_ref.dtype)


def _matmul(xs, w, out_dtype, *, bm, bn, n_col_blocks=None, w_transposed=False, name="matmul"):
    m = xs[0].shape[0]
    kp = xs[0].shape[1]
    n_parts = len(xs)
    k_axis, n_axis = (1, 0) if w_transposed else (0, 1)
    assert all(x.shape == (m, kp) for x in xs) and w.shape[k_axis] == n_parts * kp
    if n_col_blocks is None:
        n_col_blocks = w.shape[n_axis] // bn
    w_block = (bn, kp) if w_transposed else (kp, bn)

    def w_spec(p):
        return pl.BlockSpec(w_block, (lambda j, i: (j, p)) if w_transposed else (lambda j, i: (p, j)))

    return pl.pallas_call(
        functools.partial(_matmul_kernel, n_parts=n_parts, w_transposed=w_transposed),
        grid=(n_col_blocks, m // bm),
        in_specs=[pl.BlockSpec((bm, kp), lambda j, i: (i, 0)) for _ in xs]
                 + [w_spec(p) for p in range(n_parts)],
        out_specs=pl.BlockSpec((bm, bn), lambda j, i: (i, j)),
        out_shape=jax.ShapeDtypeStruct((m, n_col_blocks * bn), out_dtype),
        scratch_shapes=[pltpu.VMEM(w_block, BF16) for _ in xs],
        compiler_params=_params(2),
        name=name,
    )(*xs, *([w] * n_parts))


def _forget_proj_kernel(x_ref, wt_ref, af_ref, aft_ref):
    w16 = wt_ref[...].astype(BF16)
    x = x_ref[...]
    af_ref[...] = _dot_nt(x, w16)
    aft_ref[...] = _dot_nt(w16, x)


def _forget_proj(x, w_t, row0, n_rows, bm=1024):
    m, k = x.shape
    assert row0 % n_rows == 0 and w_t.shape[1] == k
    return pl.pallas_call(
        _forget_proj_kernel,
        grid=(m // bm,),
        in_specs=[pl.BlockSpec((bm, k), lambda i: (i, 0)),
                  pl.BlockSpec((n_rows, k), lambda i: (row0 // n_rows, 0))],
        out_specs=[pl.BlockSpec((bm, n_rows), lambda i: (i, 0)),
                   pl.BlockSpec((n_rows, bm), lambda i: (0, i))],
        out_shape=[jax.ShapeDtypeStruct((m, n_rows), F32),
                   jax.ShapeDtypeStruct((n_rows, m), F32)],
        compiler_params=_params(1),
        name="forget_proj",
    )(x, w_t)


def _forget_cumsum_kernel(af_ref, aft_ref, bcol_ref, brow_ref, ccol_ref, crow_ref, *, tc):
    s_len, nh = af_ref.shape
    r = lax.broadcasted_iota(jnp.int32, (tc, tc), 0)
    c = lax.broadcasted_iota(jnp.int32, (tc, tc), 1)
    lower = (c <= r).astype(F32).astype(BF16)
    upper = (r <= c).astype(F32).astype(BF16)

    def body(i, carry):
        carry_row, carry_col = carry
        t0 = pl.multiple_of(i * tc, tc)
        lf = _log_sigmoid(af_ref[pl.ds(t0, tc), :] + brow_ref[...])
        lft = _log_sigmoid(aft_ref[:, pl.ds(t0, tc)] + bcol_ref[...])
        a, b, d = _split3(lf)
        cc = (_dot(lower, a) + _dot(lower, b) + _dot(lower, d)) + carry_row
        a, b, d = _split3(lft)
        cr = (_dot(a, upper) + _dot(b, upper) + _dot(d, upper)) + carry_col
        ccol_ref[pl.ds(t0, tc), :] = cc * LOG2E
        crow_ref[:, pl.ds(t0, tc)] = cr * LOG2E
        return cc[tc - 1:tc, :], cr[:, tc - 1:tc]

    lax.fori_loop(0, s_len // tc, body,
                  (jnp.zeros((1, nh), F32), jnp.zeros((nh, 1), F32)))


def _forget_cumsum(af, aft, b_f, bsz, tc=256):
    m, nh = af.shape
    s_len = m // bsz
    return pl.pallas_call(
        functools.partial(_forget_cumsum_kernel, tc=tc),
        grid=(bsz,),
        in_specs=[pl.BlockSpec((s_len, nh), lambda b: (b, 0)),
                  pl.BlockSpec((nh, s_len), lambda b: (0, b)),
                  pl.BlockSpec((nh, 1), lambda b: (0, 0)),
                  pl.BlockSpec((1, nh), lambda b: (0, 0))],
        out_specs=[pl.BlockSpec((s_len, nh), lambda b: (b, 0)),
                   pl.BlockSpec((nh, s_len), lambda b: (0, b))],
        out_shape=[jax.ShapeDtypeStruct((m, nh), F32),
                   jax.ShapeDtypeStruct((nh, m), F32)],
        compiler_params=_params(1),
        name="forget_cumsum",
    )(af, aft, b_f.reshape(nh, 1), b_f.reshape(1, nh))


def _fox_kernel(q_ref, k_ref, v_ref, g_ref, crow_ref, ccol_ref, o_ref, *, tq, tk, scale2, hp):
    h0 = pl.program_id(1) * hp
    s_len = q_ref.shape[0]
    lane = lax.broadcasted_iota(jnp.int32, (tq, ccol_ref.shape[-1]), 1)
    row = lax.broadcasted_iota(jnp.int32, (tq, tk), 0)
    col = lax.broadcasted_iota(jnp.int32, (tq, tk), 1)
    heads = [slice(i * HEAD_DIM, (i + 1) * HEAD_DIM) for i in range(hp)]

    def q_body(qi, _):
        q0 = pl.multiple_of(qi * tq, tq)
        qs = [q_ref[pl.ds(q0, tq), hs] for hs in heads]
        c_blk = ccol_ref[pl.ds(q0, tq), :]
        c_ts = [jnp.sum(jnp.where(lane == h0 + i, c_blk, 0.0), axis=1, keepdims=True) for i in range(hp)]
        n_full = q0 // tk

        def logits(k0):
            return tuple(_dot_nt(qs[i], k_ref[pl.ds(k0, tk), hs]) * scale2 - crow_ref[i, :, pl.ds(k0, tk)]
                         for i, hs in enumerate(heads))

        def absorb(us, k0, state, masked):
            if masked:
                us = [jnp.where(col + k0 <= row + q0, u, NEG) for u in us]
            m_news = [jnp.maximum(st[0], jnp.max(u, axis=1, keepdims=True) + c_t)
                      for u, st, c_t in zip(us, state, c_ts)]
            ps = [jnp.exp2(u - (m_new - c_t)) for u, m_new, c_t in zip(us, m_news, c_ts)]
            pvs = [_dot(p.astype(BF16), v_ref[pl.ds(k0, tk), hs]) for p, hs in zip(ps, heads)]
            out = []
            for (m, l, acc), m_new, p, pv in zip(state, m_news, ps, pvs):
                alpha = jnp.exp2(m - m_new)
                out.append((m_new, alpha * l + jnp.sum(p, axis=1, keepdims=True), alpha * acc + pv))
            return tuple(out)

        def step(kj, state):
            k0 = pl.multiple_of(kj * tk, tk)
            return absorb(logits(k0), k0, state, False)

        state = tuple((jnp.full((tq, 1), NEG, F32), jnp.zeros((tq, 1), F32), jnp.zeros((tq, HEAD_DIM), F32))
                      for _ in heads)
        state = lax.fori_loop(0, n_full, step, state)
        k_diag = pl.multiple_of(n_full * tk, tk)
        state = absorb(logits(k_diag), k_diag, state, True)
        for i, hs in enumerate(heads):
            _, l, acc = state[i]
            g = g_ref[pl.ds(q0, tq), hs].astype(F32)
            o_ref[pl.ds(q0, tq), hs] = ((acc / l) * _silu(g)).astype(o_ref.dtype)
        return 0

    lax.fori_loop(0, s_len // tq, q_body, 0)


def _fox_attention(proj, c_row, c_col, bsz, s_len, n_heads, col_blocks, tq=256, tk=1024, hp=2):
    qb, kb, vb, gb = col_blocks
    scale2 = LOG2E * HEAD_DIM ** -0.5
    assert n_heads % hp == 0 and all(off % hp == 0 for off in col_blocks)
    assert tk % tq == 0 and s_len % tk == 0

    def head_spec(off):
        return pl.BlockSpec((s_len, hp * HEAD_DIM), lambda b, h: (b, off // hp + h))

    return pl.pallas_call(
        functools.partial(_fox_kernel, tq=tq, tk=tk, scale2=scale2, hp=hp),
        grid=(bsz, n_heads // hp),
        in_specs=[head_spec(qb), head_spec(kb), head_spec(vb), head_spec(gb),
                  pl.BlockSpec((hp, 1, s_len), lambda b, h: (h, 0, b)),
                  pl.BlockSpec((s_len, n_heads), lambda b, h: (b, 0))],
        out_specs=pl.BlockSpec((s_len, hp * HEAD_DIM), lambda b, h: (b, h)),
        out_shape=jax.ShapeDtypeStruct((bsz * s_len, n_heads * HEAD_DIM), BF16),
        compiler_params=_params(2),
        name="fox_attention",
    )(proj, proj, proj, proj, c_row.reshape(n_heads, 1, bsz * s_len), c_col)


def _chunk_kernel(q_ref, k_ref, v_ref, g_ref, rel_ref, o_ref, kpad, vpad, bias_ref, *, tq, scale, hp):
    b = pl.program_id(1)
    s_len = q_ref.shape[0]
    pad = LEFT_CHUNKS * CHUNK
    win = tq + pad
    rp = rel_ref.shape[-1]
    heads = [slice(i * HEAD_DIM, (i + 1) * HEAD_DIM) for i in range(hp)]

    @pl.when(b == 0)
    def _build_bias():
        u = lax.broadcasted_iota(jnp.int32, (rp, win), 1)
        r = lax.broadcasted_iota(jnp.int32, (rp, win), 0)
        idx = jnp.where(u < pad + CHUNK, jnp.clip(pad - u, -REL_CLIP, REL_CLIP) + REL_CLIP, 2 * REL_CLIP)
        sel = (r == idx).astype(F32).astype(BF16)
        qc = lax.broadcasted_iota(jnp.int32, (tq, win), 0) // CHUNK
        kc = lax.broadcasted_iota(jnp.int32, (tq, win), 1) // CHUNK
        in_band = (kc >= qc) & (kc <= qc + LEFT_CHUNKS)
        for i in range(hp):
            a, bb, d = _split3(jnp.broadcast_to(rel_ref[i], (8, rp)))
            g = _dot(a, sel) + _dot(bb, sel) + _dot(d, sel)
            tile = pltpu.roll(jnp.broadcast_to(g[0:1, :], (tq, win)), 0, 1, stride=1, stride_axis=0)
            bias_ref[i] = jnp.where(in_band, tile, NEG)

    kpad[pl.ds(0, pad), :] = jnp.zeros((pad, hp * HEAD_DIM), kpad.dtype)
    vpad[pl.ds(0, pad), :] = jnp.zeros((pad, hp * HEAD_DIM), vpad.dtype)
    kpad[pl.ds(pad, s_len), :] = k_ref[...]
    vpad[pl.ds(pad, s_len), :] = v_ref[...]
    key_pos = lax.broadcasted_iota(jnp.int32, (tq, win), 1)

    def body(j, _, before_start):
        q0 = pl.multiple_of(j * tq, tq)
        ss = [_dot_nt(q_ref[pl.ds(q0, tq), hs], kpad[pl.ds(q0, win), hs]) * scale + bias_ref[i]
              for i, hs in enumerate(heads)]
        if before_start:
            ss = [jnp.where(key_pos + q0 >= pad, s, NEG) for s in ss]
        ps = [jnp.exp(s - jnp.max(s, axis=1, keepdims=True)) for s in ss]
        pvs = [_dot(p.astype(BF16), vpad[pl.ds(q0, win), hs]) for p, hs in zip(ps, heads)]
        for p, pv, hs in zip(ps, pvs, heads):
            o = pv / jnp.sum(p, axis=1, keepdims=True)
            g = g_ref[pl.ds(q0, tq), hs].astype(F32)
            o_ref[pl.ds(q0, tq), hs] = (o * _silu(g)).astype(o_ref.dtype)
        return 0

    n_start = pad // tq
    lax.fori_loop(0, n_start, functools.partial(body, before_start=True), 0)
    lax.fori_loop(n_start, s_len // tq, functools.partial(body, before_start=False), 0)


def _chunk_attention(proj, rel_bias, bsz, s_len, n_heads, col_blocks, tq=256, hp=2):
    qb, kb, vb, gb = col_blocks
    scale = HEAD_DIM ** -0.5
    pad = LEFT_CHUNKS * CHUNK
    rp = 3 * LANES
    assert N_REL <= rp and pad % tq == 0
    assert n_heads % hp == 0 and all(off % hp == 0 for off in col_blocks)
    rel = jnp.pad(rel_bias, ((0, 0), (0, rp - N_REL))).reshape(n_heads, 1, rp)

    def head_spec(off):
        return pl.BlockSpec((s_len, hp * HEAD_DIM), lambda h, b: (b, off // hp + h))

    return pl.pallas_call(
        functools.partial(_chunk_kernel, tq=tq, scale=scale, hp=hp),
        grid=(n_heads // hp, bsz),
        in_specs=[head_spec(qb), head_spec(kb), head_spec(vb), head_spec(gb),
                  pl.BlockSpec((hp, 1, rp), lambda h, b: (h, 0, 0))],
        out_specs=pl.BlockSpec((s_len, hp * HEAD_DIM), lambda h, b: (b, h)),
        out_shape=jax.ShapeDtypeStruct((bsz * s_len, n_heads * HEAD_DIM), BF16),
        scratch_shapes=[pltpu.VMEM((s_len + pad, hp * HEAD_DIM), BF16),
                        pltpu.VMEM((s_len + pad, hp * HEAD_DIM), BF16),
                        pltpu.VMEM((hp, tq, tq + pad), F32)],
        compiler_params=_params(2),
        name="chunk_attention",
    )(proj, proj, proj, proj, rel)


def _sb_kernel(q_ref, k_ref, v_ref, g_ref, o_ref, *, tq, scale2, hp):
    s_len = q_ref.shape[0]
    row = lax.broadcasted_iota(jnp.int32, (tq, tq), 0)
    col = lax.broadcasted_iota(jnp.int32, (tq, tq), 1)
    strict = col < row
    after = (row > col).astype(F32).astype(BF16)
    after2 = jnp.concatenate([after, after], axis=0)
    heads = [slice(i * HEAD_DIM, (i + 1) * HEAD_DIM) for i in range(hp)]

    def log2_terms(q, k):
        z = _dot_nt(q, k) * scale2
        lb = jnp.minimum(z, 0.0) - jnp.log2(1.0 + jnp.exp2(-jnp.abs(z)))
        return lb, lb - z

    def suffix_sums(lom):
        hi = lom.astype(BF16)
        lo = (lom - hi.astype(F32)).astype(BF16)
        tail = _dot(jnp.concatenate([hi, lo], axis=1), after2)
        return tail, tail[:, 0:1] + lom[:, 0:1]

    def finish(q0, hs, acc):
        g = g_ref[pl.ds(q0, tq), hs].astype(F32)
        o_ref[pl.ds(q0, tq), hs] = (acc * _silu(g)).astype(o_ref.dtype)

    def diag_tile(q, q0, hs):
        lb, lom = log2_terms(q, k_ref[pl.ds(q0, tq), hs])
        lom = jnp.where(strict, lom, 0.0)
        tail, rsum = suffix_sums(lom)
        a = jnp.where(strict, jnp.exp2(lb + tail), 0.0)
        return a, rsum

    for hs in heads:
        a, _ = diag_tile(q_ref[pl.ds(0, tq), hs], 0, hs)
        finish(0, hs, _dot(a.astype(BF16), v_ref[pl.ds(0, tq), hs]))

    def more(carry):
        j, rsum, _ = carry
        return jnp.logical_and(j >= 0, jnp.max(rsum) >= SB_STOP_LOG2)

    def q_body(qi, _):
        q0 = pl.multiple_of(qi * tq, tq)
        k1 = pl.multiple_of(q0 - tq, tq)
        qs = [q_ref[pl.ds(q0, tq), hs] for hs in heads]
        terms = [log2_terms(q, k_ref[pl.ds(k1, 2 * tq), hs]) for q, hs in zip(qs, heads)]
        sums = []
        for lb, lom in terms:
            tail_d, rsum_d = suffix_sums(jnp.where(strict, lom[:, tq:], 0.0))
            tail_l, rsum_l = suffix_sums(lom[:, :tq])
            sums.append((tail_l, tail_d, rsum_d, rsum_d + rsum_l))
        accs = []
        for (lb, _), (tail_l, tail_d, rsum_d, _), hs in zip(terms, sums, heads):
            a_l = jnp.exp2(lb[:, :tq] + tail_l + rsum_d)
            a_d = jnp.where(strict, jnp.exp2(lb[:, tq:] + tail_d), 0.0)
            a = jnp.concatenate([a_l, a_d], axis=1).astype(BF16)
            accs.append(_dot(a, v_ref[pl.ds(k1, 2 * tq), hs]))
        gates = [_silu(g_ref[pl.ds(q0, tq), hs].astype(F32)) for hs in heads]

        outs = []
        for hs, q, (_, _, _, rsum), acc in zip(heads, qs, sums, accs):
            def tile(carry, q=q, hs=hs):
                j, rsum, acc = carry
                k0 = pl.multiple_of(j * tq, tq)
                lb, lom = log2_terms(q, k_ref[pl.ds(k0, tq), hs])
                tail, rs = suffix_sums(lom)
                a = jnp.exp2(lb + tail + rsum)
                acc = acc + _dot(a.astype(BF16), v_ref[pl.ds(k0, tq), hs])
                return j - 1, rsum + rs, acc

            outs.append(lax.while_loop(more, tile, (qi - 2, rsum, acc))[2])
        for hs, acc, gate in zip(heads, outs, gates):
            o_ref[pl.ds(q0, tq), hs] = (acc * gate).astype(o_ref.dtype)
        return 0

    lax.fori_loop(1, s_len // tq, q_body, 0)


def _sb_attention(proj, bsz, s_len, n_heads, col_blocks, tq=256, hp=4):
    qb, kb, vb, gb = col_blocks
    scale2 = LOG2E * HEAD_DIM ** -0.5
    assert n_heads % hp == 0 and all(off % hp == 0 for off in col_blocks)

    def head_spec(off):
        return pl.BlockSpec((s_len, hp * HEAD_DIM), lambda b, h: (b, off // hp + h))

    return pl.pallas_call(
        functools.partial(_sb_kernel, tq=tq, scale2=scale2, hp=hp),
        grid=(bsz, n_heads // hp),
        in_specs=[head_spec(qb), head_spec(kb), head_spec(vb), head_spec(gb)],
        out_specs=pl.BlockSpec((s_len, hp * HEAD_DIM), lambda b, h: (b, h)),
        out_shape=jax.ShapeDtypeStruct((bsz * s_len, n_heads * HEAD_DIM), BF16),
        compiler_params=_params(2),
        name="sb_attention",
    )(proj, proj, proj, proj)


def _even_layer(x2, bsz, s_len, g_pre, g_post, w_in, b_f, rel_bias, w_out):
    d = x2.shape[1]
    n_a = b_f.shape[0]
    n_b = rel_bias.shape[0]
    h = _rmsnorm(x2, g_pre, BF16)
    main_cols = 4 * (n_a + n_b) * HEAD_DIM
    w_in_t = w_in.T
    proj = _matmul([h], w_in_t, BF16, bm=PROJ_BM, bn=PROJ_BN, n_col_blocks=main_cols // PROJ_BN,
                   w_transposed=True, name="in_proj_even")
    af, aft = _forget_proj(h, w_in_t, main_cols, n_a)
    c_col, c_row = _forget_cumsum(af, aft, b_f, bsz)
    mix_a = _fox_attention(proj, c_row, c_col, bsz, s_len, n_a,
                           (0, n_a, 2 * n_a, 3 * n_a))
    off = 4 * n_a
    mix_b = _chunk_attention(proj, rel_bias, bsz, s_len, n_b,
                             (off, off + n_b, off + 2 * n_b, off + 3 * n_b))
    y = _matmul([mix_a, mix_b], w_out, F32, bm=PROJ_BM, bn=PROJ_BN, name="out_proj_even")
    return _postnorm_residual(x2, y, g_post)


def _odd_layer(x2, bsz, s_len, g_pre, g_post, w_in, w_out):
    n_c = w_out.shape[0] // HEAD_DIM
    h = _rmsnorm(x2, g_pre, BF16)
    proj = _matmul([h], w_in, BF16, bm=PROJ_BM, bn=PROJ_BN, name="in_proj_odd")
    mix = _sb_attention(proj, bsz, s_len, n_c, (0, n_c, 2 * n_c, 3 * n_c))
    y = _matmul([mix], w_out, F32, bm=PROJ_BM, bn=PROJ_BN, name="out_proj_odd")
    return _postnorm_residual(x2, y, g_post)


def kernel(x, norm_pre, norm_post, w_in_even, b_f_even, rel_bias_even, w_out_even, w_in_odd, w_out_odd):
    bsz, s_len, d = x.shape
    depth = norm_pre.shape[0]
    x2 = x.reshape(bsz * s_len, d)
    for layer in range(depth):
        i = layer // 2
        if layer % 2 == 0:
            x2 = _even_layer(x2, bsz, s_len, norm_pre[layer], norm_post[layer],
                             w_in_even[i], b_f_even[i], rel_bias_even[i], w_out_even[i])
        else:
            x2 = _odd_layer(x2, bsz, s_len, norm_pre[layer], norm_post[layer],
                            w_in_odd[i], w_out_odd[i])
    return x2.reshape(bsz, s_len, d)
```

```python
import functools

import jax
import jax.numpy as jnp
from jax import lax
from jax.experimental import pallas as pl
from jax.experimental.pallas import tpu as pltpu

HEAD_DIM = 128
CHUNK = 64
LEFT_CHUNKS = 8
REL_CLIP = 128
N_REL = 2 * REL_CLIP + 1
RMS_EPS = 1e-6

LANES = 128
VMEM_LIMIT = 60 * 1024 * 1024
PROJ_BM, PROJ_BN = 512, 1024
OUT_BM, OUT_BN = 512, 512
NORM_ROWS = 128
NEG = -1e30
LOG2E = 1.4426950408889634
SB_STOP_LOG2 = -152.0

BF16 = jnp.bfloat16
F32 = jnp.float32


def _params(n_axes):
    return pltpu.CompilerParams(dimension_semantics=("arbitrary",) * n_axes,
                                vmem_limit_bytes=VMEM_LIMIT)


def _dot_nt(a, b):
    return lax.dot_general(a, b, (((1,), (1,)), ((), ())), preferred_element_type=F32)


def _dot(a, b):
    return jnp.dot(a, b, preferred_element_type=F32)


def _split3(x):
    hi = x.astype(BF16)
    r1 = x - hi.astype(F32)
    mid = r1.astype(BF16)
    lo = (r1 - mid.astype(F32)).astype(BF16)
    return hi, mid, lo


def _log_sigmoid(z):
    return jnp.minimum(z, 0.0) - jnp.log1p(jnp.exp(-jnp.abs(z)))


def _silu(g):
    return g / (1.0 + jnp.exp(-g))


def _rmsnorm_kernel(x_ref, g_ref, o_ref):
    x = x_ref[...]
    ms = jnp.mean(x * x, axis=-1, keepdims=True)
    o_ref[...] = (x * lax.rsqrt(ms + RMS_EPS) * g_ref[...]).astype(o_ref.dtype)


def _rmsnorm(x, g, out_dtype, bm=256):
    m, d = x.shape
    return pl.pallas_call(
        _rmsnorm_kernel,
        grid=(m // bm,),
        in_specs=[pl.BlockSpec((bm, d), lambda i: (i, 0)),
                  pl.BlockSpec((1, d), lambda i: (0, 0))],
        out_specs=pl.BlockSpec((bm, d), lambda i: (i, 0)),
        out_shape=jax.ShapeDtypeStruct((m, d), out_dtype),
        compiler_params=_params(1),
        name="rmsnorm",
    )(x, g.reshape(1, d))


def _out_proj_norm_kernel(*refs, n_parts, bn, with_next):
    mix_refs, w_refs = refs[:n_parts], refs[n_parts:2 * n_parts]
    rest = list(refs[2 * n_parts:])
    xres_ref, gpost_ref = rest.pop(0), rest.pop(0)
    gnext_ref = rest.pop(0) if with_next else None
    o_ref = rest.pop(0)
    h_ref = rest.pop(0) if with_next else None
    xs_ref = rest.pop(0)
    j = pl.program_id(1)
    c0 = pl.multiple_of(j * bn, bn)

    y = _dot(mix_refs[0][...], w_refs[0][...])
    for mix_ref, w_ref in zip(mix_refs[1:], w_refs[1:]):
        y = y + _dot(mix_ref[...], w_ref[...])
    o_ref[:, pl.ds(c0, bn)] = y
    xs_ref[:, pl.ds(c0, bn)] = xres_ref[...]

    @pl.when(j == pl.num_programs(1) - 1)
    def _normalise():
        def rows(r, _):
            rs = pl.ds(pl.multiple_of(r * NORM_ROWS, NORM_ROWS), NORM_ROWS)
            y = o_ref[rs, :]
            ms = jnp.mean(y * y, axis=-1, keepdims=True)
            x_new = xs_ref[rs, :] + y * lax.rsqrt(ms + RMS_EPS) * gpost_ref[...]
            o_ref[rs, :] = x_new
            if with_next:
                ms = jnp.mean(x_new * x_new, axis=-1, keepdims=True)
                h_ref[rs, :] = (x_new * lax.rsqrt(ms + RMS_EPS) * gnext_ref[...]).astype(h_ref.dtype)
            return 0

        lax.fori_loop(0, o_ref.shape[0] // NORM_ROWS, rows, 0)


def _out_proj_norm(mixes, w16, x, g_post, g_next, *, bm, bn, name):
    m, d = x.shape
    kp = mixes[0].shape[1]
    n_parts = len(mixes)
    with_next = g_next is not None
    assert all(t.shape == (m, kp) for t in mixes) and w16.shape == (n_parts * kp, d)
    row_block = pl.BlockSpec((bm, d), lambda i, j: (i, 0))
    gain = pl.BlockSpec((1, d), lambda i, j: (0, 0))
    in_specs = ([pl.BlockSpec((bm, kp), lambda i, j: (i, 0)) for _ in mixes]
                + [pl.BlockSpec((kp, bn), lambda i, j, p=p: (p, j)) for p in range(n_parts)]
                + [pl.BlockSpec((bm, bn), lambda i, j: (i, j)), gain] + ([gain] if with_next else []))
    operands = list(mixes) + [w16] * n_parts + [x, g_post.reshape(1, d)]
    out_specs, out_shape = [row_block], [jax.ShapeDtypeStruct((m, d), F32)]
    if with_next:
        operands.append(g_next.reshape(1, d))
        out_specs.append(row_block)
        out_shape.append(jax.ShapeDtypeStruct((m, d), BF16))
    outs = pl.pallas_call(
        functools.partial(_out_proj_norm_kernel, n_parts=n_parts, bn=bn, with_next=with_next),
        grid=(m // bm, d // bn),
        in_specs=in_specs,
        out_specs=out_specs,
        out_shape=out_shape,
        scratch_shapes=[pltpu.VMEM((bm, d), F32)],
        compiler_params=_params(2),
        name=name,
    )(*operands)
    return (outs[0], outs[1]) if with_next else (outs[0], None)


def _matmul_kernel(*refs, n_parts, w_transposed):
    x_refs, w_refs = refs[:n_parts], refs[n_parts:2 * n_parts]
    o_ref = refs[2 * n_parts]
    w16_refs = refs[2 * n_parts + 1:]
    dot = _dot_nt if w_transposed else _dot

    @pl.when(pl.program_id(1) == 0)
    def _round_weights():
        for w_ref, w16_ref in zip(w_refs, w16_refs):
            w16_ref[...] = w_ref[...].astype(BF16)

    acc = dot(x_refs[0][...], w16_refs[0][...])
    for x_ref, w16_ref in zip(x_refs[1:], w16_refs[1:]):
        acc = acc + dot(x_ref[...], w16_ref[...])
    o_ref[...] = acc.astype(o_ref.dtype)


def _matmul(xs, w, out_dtype, *, bm, bn, n_col_blocks=None, w_transposed=False, name="matmul"):
    m = xs[0].shape[0]
    kp = xs[0].shape[1]
    n_parts = len(xs)
    k_axis, n_axis = (1, 0) if w_transposed else (0, 1)
    assert all(x.shape == (m, kp) for x in xs) and w.shape[k_axis] == n_parts * kp
    if n_col_blocks is None:
        n_col_blocks = w.shape[n_axis] // bn
    w_block = (bn, kp) if w_transposed else (kp, bn)

    def w_spec(p):
        return pl.BlockSpec(w_block, (lambda j, i: (j, p)) if w_transposed else (lambda j, i: (p, j)))

    return pl.pallas_call(
        functools.partial(_matmul_kernel, n_parts=n_parts, w_transposed=w_transposed),
        grid=(n_col_blocks, m // bm),
        in_specs=[pl.BlockSpec((bm, kp), lambda j, i: (i, 0)) for _ in xs]
                 + [w_spec(p) for p in range(n_parts)],
        out_specs=pl.BlockSpec((bm, bn), lambda j, i: (i, j)),
        out_shape=jax.ShapeDtypeStruct((m, n_col_blocks * bn), out_dtype),
        scratch_shapes=[pltpu.VMEM(w_block, BF16) for _ in xs],
        compiler_params=_params(2),
        name=name,
    )(*xs, *([w] * n_parts))


def _forget_proj_kernel(x_ref, wt_ref, af_ref, aft_ref):
    w16 = wt_ref[...].astype(BF16)
    x = x_ref[...]
    af_ref[...] = _dot_nt(x, w16)
    aft_ref[...] = _dot_nt(w16, x)


def _forget_proj(x, w_t, row0, n_rows, bm=1024):
    m, k = x.shape
    assert row0 % n_rows == 0 and w_t.shape[1] == k
    return pl.pallas_call(
        _forget_proj_kernel,
        grid=(m // bm,),
        in_specs=[pl.BlockSpec((bm, k), lambda i: (i, 0)),
                  pl.BlockSpec((n_rows, k), lambda i: (row0 // n_rows, 0))],
        out_specs=[pl.BlockSpec((bm, n_rows), lambda i: (i, 0)),
                   pl.BlockSpec((n_rows, bm), lambda i: (0, i))],
        out_shape=[jax.ShapeDtypeStruct((m, n_rows), F32),
                   jax.ShapeDtypeStruct((n_rows, m), F32)],
        compiler_params=_params(1),
        name="forget_proj",
    )(x, w_t)


def _forget_cumsum_kernel(af_ref, aft_ref, bcol_ref, brow_ref, ccol_ref, crow_ref, *, tc):
    s_len, nh = af_ref.shape
    r = lax.broadcasted_iota(jnp.int32, (tc, tc), 0)
    c = lax.broadcasted_iota(jnp.int32, (tc, tc), 1)
    lower = (c <= r).astype(F32).astype(BF16)
    upper = (r <= c).astype(F32).astype(BF16)

    def body(i, carry):
        carry_row, carry_col = carry
        t0 = pl.multiple_of(i * tc, tc)
        lf = _log_sigmoid(af_ref[pl.ds(t0, tc), :] + brow_ref[...])
        lft = _log_sigmoid(aft_ref[:, pl.ds(t0, tc)] + bcol_ref[...])
        a, b, d = _split3(lf)
        cc = (_dot(lower, a) + _dot(lower, b) + _dot(lower, d)) + carry_row
        a, b, d = _split3(lft)
        cr = (_dot(a, upper) + _dot(b, upper) + _dot(d, upper)) + carry_col
        ccol_ref[pl.ds(t0, tc), :] = cc * LOG2E
        crow_ref[:, pl.ds(t0, tc)] = cr * LOG2E
        return cc[tc - 1:tc, :], cr[:, tc - 1:tc]

    lax.fori_loop(0, s_len // tc, body,
                  (jnp.zeros((1, nh), F32), jnp.zeros((nh, 1), F32)))


def _forget_cumsum(af, aft, b_f, bsz, tc=256):
    m, nh = af.shape
    s_len = m // bsz
    return pl.pallas_call(
        functools.partial(_forget_cumsum_kernel, tc=tc),
        grid=(bsz,),
        in_specs=[pl.BlockSpec((s_len, nh), lambda b: (b, 0)),
                  pl.BlockSpec((nh, s_len), lambda b: (0, b)),
                  pl.BlockSpec((nh, 1), lambda b: (0, 0)),
                  pl.BlockSpec((1, nh), lambda b: (0, 0))],
        out_specs=[pl.BlockSpec((s_len, nh), lambda b: (b, 0)),
                   pl.BlockSpec((nh, s_len), lambda b: (0, b))],
        out_shape=[jax.ShapeDtypeStruct((m, nh), F32),
                   jax.ShapeDtypeStruct((nh, m), F32)],
        compiler_params=_params(1),
        name="forget_cumsum",
    )(af, aft, b_f.reshape(nh, 1), b_f.reshape(1, nh))


def _fox_kernel(q_ref, k_ref, v_ref, g_ref, crow_ref, ccol_ref, o_ref, *, tq, tk, scale2, hp):
    h0 = pl.program_id(1) * hp
    s_len = q_ref.shape[0]
    lane = lax.broadcasted_iota(jnp.int32, (tq, ccol_ref.shape[-1]), 1)
    row = lax.broadcasted_iota(jnp.int32, (tq, tk), 0)
    col = lax.broadcasted_iota(jnp.int32, (tq, tk), 1)
    heads = [slice(i * HEAD_DIM, (i + 1) * HEAD_DIM) for i in range(hp)]

    def q_body(qi, _):
        q0 = pl.multiple_of(qi * tq, tq)
        qs = [q_ref[pl.ds(q0, tq), hs] for hs in heads]
        c_blk = ccol_ref[pl.ds(q0, tq), :]
        c_ts = [jnp.sum(jnp.where(lane == h0 + i, c_blk, 0.0), axis=1, keepdims=True) for i in range(hp)]
        n_full = q0 // tk

        def logits(k0, width):
            return tuple(_dot_nt(qs[i], k_ref[pl.ds(k0, width), hs]) * scale2 - crow_ref[i, :, pl.ds(k0, width)]
                         for i, hs in enumerate(heads))

        def absorb(us, k0, width, state, diag_offset=None):
            if diag_offset is not None:
                keep = (lax.broadcasted_iota(jnp.int32, (tq, width), 1)
                        <= lax.broadcasted_iota(jnp.int32, (tq, width), 0) + diag_offset)
                us = [jnp.where(keep, u, NEG) for u in us]
            m_news = [jnp.maximum(st[0], jnp.max(u, axis=1, keepdims=True) + c_t)
                      for u, st, c_t in zip(us, state, c_ts)]
            ps = [jnp.exp2(u - (m_new - c_t)) for u, m_new, c_t in zip(us, m_news, c_ts)]
            pvs = [_dot(p.astype(BF16), v_ref[pl.ds(k0, width), hs]) for p, hs in zip(ps, heads)]
            out = []
            for (m, l, acc), m_new, p, pv in zip(state, m_news, ps, pvs):
                alpha = jnp.exp2(m - m_new)
                out.append((m_new, alpha * l + jnp.sum(p, axis=1, keepdims=True), alpha * acc + pv))
            return tuple(out)

        def step(kj, state):
            k0 = pl.multiple_of(kj * tk, tk)
            return absorb(logits(k0, tk), k0, tk, state)

        state = tuple((jnp.full((tq, 1), NEG, F32), jnp.zeros((tq, 1), F32), jnp.zeros((tq, HEAD_DIM), F32))
                      for _ in heads)
        state = lax.fori_loop(0, n_full, step, state)
        k_diag = pl.multiple_of(n_full * tk, tk)

        def diag_branch(r):
            width = (r + 1) * tq
            return lambda st: absorb(logits(k_diag, width), k_diag, width, st, diag_offset=r * tq)

        state = lax.switch(qi % (tk // tq), [diag_branch(r) for r in range(tk // tq)], state)
        for i, hs in enumerate(heads):
            _, l, acc = state[i]
            g = g_ref[pl.ds(q0, tq), hs].astype(F32)
            o_ref[pl.ds(q0, tq), hs] = ((acc / l) * _silu(g)).astype(o_ref.dtype)
        return 0

    lax.fori_loop(0, s_len // tq, q_body, 0)


def _fox_attention(proj, c_row, c_col, bsz, s_len, n_heads, col_blocks, tq=256, tk=1024, hp=2):
    qb, kb, vb, gb = col_blocks
    scale2 = LOG2E * HEAD_DIM ** -0.5
    assert n_heads % hp == 0 and all(off % hp == 0 for off in col_blocks)
    assert tk % tq == 0 and s_len % tk == 0

    def head_spec(off):
        return pl.BlockSpec((s_len, hp * HEAD_DIM), lambda b, h: (b, off // hp + h))

    return pl.pallas_call(
        functools.partial(_fox_kernel, tq=tq, tk=tk, scale2=scale2, hp=hp),
        grid=(bsz, n_heads // hp),
        in_specs=[head_spec(qb), head_spec(kb), head_spec(vb), head_spec(gb),
                  pl.BlockSpec((hp, 1, s_len), lambda b, h: (h, 0, b)),
                  pl.BlockSpec((s_len, n_heads), lambda b, h: (b, 0))],
        out_specs=pl.BlockSpec((s_len, hp * HEAD_DIM), lambda b, h: (b, h)),
        out_shape=jax.ShapeDtypeStruct((bsz * s_len, n_heads * HEAD_DIM), BF16),
        compiler_params=_params(2),
        name="fox_attention",
    )(proj, proj, proj, proj, c_row.reshape(n_heads, 1, bsz * s_len), c_col)


def _chunk_kernel(q_ref, k_ref, v_ref, g_ref, rel_ref, o_ref, kpad, vpad, bias_ref, *, tq, scale, hp):
    b = pl.program_id(1)
    s_len = q_ref.shape[0]
    pad = LEFT_CHUNKS * CHUNK
    win = tq + pad
    rp = rel_ref.shape[-1]
    heads = [slice(i * HEAD_DIM, (i + 1) * HEAD_DIM) for i in range(hp)]

    @pl.when(b == 0)
    def _build_bias():
        u = lax.broadcasted_iota(jnp.int32, (rp, win), 1)
        r = lax.broadcasted_iota(jnp.int32, (rp, win), 0)
        idx = jnp.where(u < pad + CHUNK, jnp.clip(pad - u, -REL_CLIP, REL_CLIP) + REL_CLIP, 2 * REL_CLIP)
        sel = (r == idx).astype(F32).astype(BF16)
        qc = lax.broadcasted_iota(jnp.int32, (tq, win), 0) // CHUNK
        kc = lax.broadcasted_iota(jnp.int32, (tq, win), 1) // CHUNK
        in_band = (kc >= qc) & (kc <= qc + LEFT_CHUNKS)
        for i in range(hp):
            a, bb, d = _split3(jnp.broadcast_to(rel_ref[i], (8, rp)))
            g = _dot(a, sel) + _dot(bb, sel) + _dot(d, sel)
            tile = pltpu.roll(jnp.broadcast_to(g[0:1, :], (tq, win)), 0, 1, stride=1, stride_axis=0)
            bias_ref[i] = jnp.where(in_band, tile, NEG)

    kpad[pl.ds(0, pad), :] = jnp.zeros((pad, hp * HEAD_DIM), kpad.dtype)
    vpad[pl.ds(0, pad), :] = jnp.zeros((pad, hp * HEAD_DIM), vpad.dtype)
    kpad[pl.ds(pad, s_len), :] = k_ref[...]
    vpad[pl.ds(pad, s_len), :] = v_ref[...]
    key_pos = lax.broadcasted_iota(jnp.int32, (tq, win), 1)

    def body(j, _, before_start):
        q0 = pl.multiple_of(j * tq, tq)
        ss = [_dot_nt(q_ref[pl.ds(q0, tq), hs], kpad[pl.ds(q0, win), hs]) * scale + bias_ref[i]
              for i, hs in enumerate(heads)]
        if before_start:
            ss = [jnp.where(key_pos + q0 >= pad, s, NEG) for s in ss]
        ps = [jnp.exp(s - jnp.max(s, axis=1, keepdims=True)) for s in ss]
        pvs = [_dot(p.astype(BF16), vpad[pl.ds(q0, win), hs]) for p, hs in zip(ps, heads)]
        for p, pv, hs in zip(ps, pvs, heads):
            o = pv / jnp.sum(p, axis=1, keepdims=True)
            g = g_ref[pl.ds(q0, tq), hs].astype(F32)
            o_ref[pl.ds(q0, tq), hs] = (o * _silu(g)).astype(o_ref.dtype)
        return 0

    n_start = pad // tq
    lax.fori_loop(0, n_start, functools.partial(body, before_start=True), 0)
    lax.fori_loop(n_start, s_len // tq, functools.partial(body, before_start=False), 0)


def _chunk_attention(proj, rel_bias, bsz, s_len, n_heads, col_blocks, tq=256, hp=2):
    qb, kb, vb, gb = col_blocks
    scale = HEAD_DIM ** -0.5
    pad = LEFT_CHUNKS * CHUNK
    rp = 3 * LANES
    assert N_REL <= rp and pad % tq == 0
    assert n_heads % hp == 0 and all(off % hp == 0 for off in col_blocks)
    rel = jnp.pad(rel_bias, ((0, 0), (0, rp - N_REL))).reshape(n_heads, 1, rp)

    def head_spec(off):
        return pl.BlockSpec((s_len, hp * HEAD_DIM), lambda h, b: (b, off // hp + h))

    return pl.pallas_call(
        functools.partial(_chunk_kernel, tq=tq, scale=scale, hp=hp),
        grid=(n_heads // hp, bsz),
        in_specs=[head_spec(qb), head_spec(kb), head_spec(vb), head_spec(gb),
                  pl.BlockSpec((hp, 1, rp), lambda h, b: (h, 0, 0))],
        out_specs=pl.BlockSpec((s_len, hp * HEAD_DIM), lambda h, b: (b, h)),
        out_shape=jax.ShapeDtypeStruct((bsz * s_len, n_heads * HEAD_DIM), BF16),
        scratch_shapes=[pltpu.VMEM((s_len + pad, hp * HEAD_DIM), BF16),
                        pltpu.VMEM((s_len + pad, hp * HEAD_DIM), BF16),
                        pltpu.VMEM((hp, tq, tq + pad), F32)],
        compiler_params=_params(2),
        name="chunk_attention",
    )(proj, proj, proj, proj, rel)


def _sb_kernel(q_ref, k_ref, v_ref, g_ref, o_ref, *, tq, scale2, hp):
    s_len = q_ref.shape[0]
    row = lax.broadcasted_iota(jnp.int32, (tq, tq), 0)
    col = lax.broadcasted_iota(jnp.int32, (tq, tq), 1)
    strict = col < row
    after = (row > col).astype(F32).astype(BF16)
    after2 = jnp.concatenate([after, after], axis=0)
    heads = [slice(i * HEAD_DIM, (i + 1) * HEAD_DIM) for i in range(hp)]

    def log2_terms(q, k):
        z = _dot_nt(q, k) * scale2
        lb = jnp.minimum(z, 0.0) - jnp.log2(1.0 + jnp.exp2(-jnp.abs(z)))
        return lb, lb - z

    def suffix_sums(lom):
        hi = lom.astype(BF16)
        lo = (lom - hi.astype(F32)).astype(BF16)
        tail = _dot(jnp.concatenate([hi, lo], axis=1), after2)
        return tail, tail[:, 0:1] + lom[:, 0:1]

    def finish(q0, hs, acc):
        g = g_ref[pl.ds(q0, tq), hs].astype(F32)
        o_ref[pl.ds(q0, tq), hs] = (acc * _silu(g)).astype(o_ref.dtype)

    def diag_tile(q, q0, hs):
        lb, lom = log2_terms(q, k_ref[pl.ds(q0, tq), hs])
        lom = jnp.where(strict, lom, 0.0)
        tail, rsum = suffix_sums(lom)
        a = jnp.where(strict, jnp.exp2(lb + tail), 0.0)
        return a, rsum

    for hs in heads:
        a, _ = diag_tile(q_ref[pl.ds(0, tq), hs], 0, hs)
        finish(0, hs, _dot(a.astype(BF16), v_ref[pl.ds(0, tq), hs]))

    def more(carry):
        j, rsum, _ = carry
        return jnp.logical_and(j >= 0, jnp.max(rsum) >= SB_STOP_LOG2)

    def q_body(qi, _):
        q0 = pl.multiple_of(qi * tq, tq)
        k1 = pl.multiple_of(q0 - tq, tq)
        qs = [q_ref[pl.ds(q0, tq), hs] for hs in heads]
        terms = [log2_terms(q, k_ref[pl.ds(k1, 2 * tq), hs]) for q, hs in zip(qs, heads)]
        sums = []
        for lb, lom in terms:
            tail_d, rsum_d = suffix_sums(jnp.where(strict, lom[:, tq:], 0.0))
            tail_l, rsum_l = suffix_sums(lom[:, :tq])
            sums.append((tail_l, tail_d, rsum_d, rsum_d + rsum_l))
        accs = []
        for (lb, _), (tail_l, tail_d, rsum_d, _), hs in zip(terms, sums, heads):
            a_l = jnp.exp2(lb[:, :tq] + tail_l + rsum_d)
            a_d = jnp.where(strict, jnp.exp2(lb[:, tq:] + tail_d), 0.0)
            a = jnp.concatenate([a_l, a_d], axis=1).astype(BF16)
            accs.append(_dot(a, v_ref[pl.ds(k1, 2 * tq), hs]))
        gates = [_silu(g_ref[pl.ds(q0, tq), hs].astype(F32)) for hs in heads]

        outs = []
        for hs, q, (_, _, _, rsum), acc in zip(heads, qs, sums, accs):
            def tile(carry, q=q, hs=hs):
                j, rsum, acc = carry
                k0 = pl.multiple_of(j * tq, tq)
                lb, lom = log2_terms(q, k_ref[pl.ds(k0, tq), hs])
                tail, rs = suffix_sums(lom)
                a = jnp.exp2(lb + tail + rsum)
                acc = acc + _dot(a.astype(BF16), v_ref[pl.ds(k0, tq), hs])
                return j - 1, rsum + rs, acc

            outs.append(lax.while_loop(more, tile, (qi - 2, rsum, acc))[2])
        for hs, acc, gate in zip(heads, outs, gates):
            o_ref[pl.ds(q0, tq), hs] = (acc * gate).astype(o_ref.dtype)
        return 0

    lax.fori_loop(1, s_len // tq, q_body, 0)


def _sb_attention(proj, bsz, s_len, n_heads, col_blocks, tq=256, hp=4):
    qb, kb, vb, gb = col_blocks
    scale2 = LOG2E * HEAD_DIM ** -0.5
    assert n_heads % hp == 0 and all(off % hp == 0 for off in col_blocks)

    def head_spec(off):
        return pl.BlockSpec((s_len, hp * HEAD_DIM), lambda b, h: (b, off // hp + h))

    return pl.pallas_call(
        functools.partial(_sb_kernel, tq=tq, scale2=scale2, hp=hp),
        grid=(bsz, n_heads // hp),
        in_specs=[head_spec(qb), head_spec(kb), head_spec(vb), head_spec(gb)],
        out_specs=pl.BlockSpec((s_len, hp * HEAD_DIM), lambda b, h: (b, h)),
        out_shape=jax.ShapeDtypeStruct((bsz * s_len, n_heads * HEAD_DIM), BF16),
        compiler_params=_params(2),
        name="sb_attention",
    )(proj, proj, proj, proj)


def _even_layer(x2, h, bsz, s_len, g_post, g_next, w_in, b_f, rel_bias, w_out):
    n_a = b_f.shape[0]
    n_b = rel_bias.shape[0]
    main_cols = 4 * (n_a + n_b) * HEAD_DIM
    w_in_t = w_in.T
    proj = _matmul([h], w_in_t, BF16, bm=PROJ_BM, bn=PROJ_BN, n_col_blocks=main_cols // PROJ_BN,
                   w_transposed=True, name="in_proj_even")
    af, aft = _forget_proj(h, w_in_t, main_cols, n_a)
    c_col, c_row = _forget_cumsum(af, aft, b_f, bsz)
    mix_a = _fox_attention(proj, c_row, c_col, bsz, s_len, n_a,
                           (0, n_a, 2 * n_a, 3 * n_a))
    off = 4 * n_a
    mix_b = _chunk_attention(proj, rel_bias, bsz, s_len, n_b,
                             (off, off + n_b, off + 2 * n_b, off + 3 * n_b))
    return _out_proj_norm([mix_a, mix_b], w_out.astype(BF16), x2, g_post, g_next,
                          bm=OUT_BM, bn=OUT_BN, name="out_proj_norm_even")


def _odd_layer(x2, h, bsz, s_len, g_post, g_next, w_in, w_out):
    n_c = w_out.shape[0] // HEAD_DIM
    proj = _matmul([h], w_in, BF16, bm=PROJ_BM, bn=PROJ_BN, name="in_proj_odd")
    mix = _sb_attention(proj, bsz, s_len, n_c, (0, n_c, 2 * n_c, 3 * n_c))
    return _out_proj_norm([mix], w_out.astype(BF16), x2, g_post, g_next,
                          bm=OUT_BM, bn=OUT_BN, name="out_proj_norm_odd")


def kernel(x, norm_pre, norm_post, w_in_even, b_f_even, rel_bias_even, w_out_even, w_in_odd, w_out_odd):
    bsz, s_len, d = x.shape
    depth = norm_pre.shape[0]
    x2 = x.reshape(bsz * s_len, d)
    h = _rmsnorm(x2, norm_pre[0], BF16)
    for layer in range(depth):
        i = layer // 2
        g_next = norm_pre[layer + 1] if layer + 1 < depth else None
        if layer % 2 == 0:
            x2, h = _even_layer(x2, h, bsz, s_len, norm_post[layer], g_next,
                                w_in_even[i], b_f_even[i], rel_bias_even[i], w_out_even[i])
        else:
            x2, h = _odd_layer(x2, h, bsz, s_len, norm_post[layer], g_next,
                               w_in_odd[i], w_out_odd[i])
    return x2.reshape(bsz, s_len, d)
```

```python
import functools

import jax
import jax.numpy as jnp
from jax import lax
from jax.experimental import pallas as pl
from jax.experimental.pallas import tpu as pltpu

HEAD_DIM = 128
CHUNK = 64
LEFT_CHUNKS = 8
REL_CLIP = 128
N_REL = 2 * REL_CLIP + 1
RMS_EPS = 1e-6

LANES = 128
VMEM_LIMIT = 60 * 1024 * 1024
PROJ_BM, PROJ_BN = 512, 1024
OUT_BM, OUT_BN = 512, 512
NORM_ROWS = 128
NEG = -1e30
LOG2E = 1.4426950408889634
SB_STOP_LOG2 = -152.0

BF16 = jnp.bfloat16
F32 = jnp.float32


def _params(n_axes):
    return pltpu.CompilerParams(dimension_semantics=("arbitrary",) * n_axes,
                                vmem_limit_bytes=VMEM_LIMIT)


def _dot_nt(a, b):
    return lax.dot_general(a, b, (((1,), (1,)), ((), ())), preferred_element_type=F32)


def _dot(a, b):
    return jnp.dot(a, b, preferred_element_type=F32)


def _split3(x):
    hi = x.astype(BF16)
    r1 = x - hi.astype(F32)
    mid = r1.astype(BF16)
    lo = (r1 - mid.astype(F32)).astype(BF16)
    return hi, mid, lo


def _log_sigmoid(z):
    return jnp.minimum(z, 0.0) - jnp.log1p(jnp.exp(-jnp.abs(z)))


def _silu(g):
    return g / (1.0 + jnp.exp(-g))


def _rmsnorm_kernel(x_ref, g_ref, o_ref):
    x = x_ref[...]
    ms = jnp.mean(x * x, axis=-1, keepdims=True)
    o_ref[...] = (x * lax.rsqrt(ms + RMS_EPS) * g_ref[...]).astype(o_ref.dtype)


def _rmsnorm(x, g, out_dtype, bm=256):
    m, d = x.shape
    return pl.pallas_call(
        _rmsnorm_kernel,
        grid=(m // bm,),
        in_specs=[pl.BlockSpec((bm, d), lambda i: (i, 0)),
                  pl.BlockSpec((1, d), lambda i: (0, 0))],
        out_specs=pl.BlockSpec((bm, d), lambda i: (i, 0)),
        out_shape=jax.ShapeDtypeStruct((m, d), out_dtype),
        compiler_params=_params(1),
        name="rmsnorm",
    )(x, g.reshape(1, d))


def _out_proj_norm_kernel(*refs, n_parts, bn, with_next):
    mix_refs, w_refs = refs[:n_parts], refs[n_parts:2 * n_parts]
    rest = list(refs[2 * n_parts:])
    xres_ref, gpost_ref = rest.pop(0), rest.pop(0)
    gnext_ref = rest.pop(0) if with_next else None
    o_ref = rest.pop(0)
    h_ref = rest.pop(0) if with_next else None
    xs_ref = rest.pop(0)
    j = pl.program_id(1)
    c0 = pl.multiple_of(j * bn, bn)

    y = _dot(mix_refs[0][...], w_refs[0][...])
    for mix_ref, w_ref in zip(mix_refs[1:], w_refs[1:]):
        y = y + _dot(mix_ref[...], w_ref[...])
    o_ref[:, pl.ds(c0, bn)] = y
    xs_ref[:, pl.ds(c0, bn)] = xres_ref[...]

    @pl.when(j == pl.num_programs(1) - 1)
    def _normalise():
        def rows(r, _):
            rs = pl.ds(pl.multiple_of(r * NORM_ROWS, NORM_ROWS), NORM_ROWS)
            y = o_ref[rs, :]
            ms = jnp.mean(y * y, axis=-1, keepdims=True)
            x_new = xs_ref[rs, :] + y * lax.rsqrt(ms + RMS_EPS) * gpost_ref[...]
            o_ref[rs, :] = x_new
            if with_next:
                ms = jnp.mean(x_new * x_new, axis=-1, keepdims=True)
                h_ref[rs, :] = (x_new * lax.rsqrt(ms + RMS_EPS) * gnext_ref[...]).astype(h_ref.dtype)
            return 0

        lax.fori_loop(0, o_ref.shape[0] // NORM_ROWS, rows, 0)


def _out_proj_norm(mixes, w16, x, g_post, g_next, *, bm, bn, name):
    m, d = x.shape
    kp = mixes[0].shape[1]
    n_parts = len(mixes)
    with_next = g_next is not None
    assert all(t.shape == (m, kp) for t in mixes) and w16.shape == (n_parts * kp, d)
    row_block = pl.BlockSpec((bm, d), lambda i, j: (i, 0))
    gain = pl.BlockSpec((1, d), lambda i, j: (0, 0))
    in_specs = ([pl.BlockSpec((bm, kp), lambda i, j: (i, 0)) for _ in mixes]
                + [pl.BlockSpec((kp, bn), lambda i, j, p=p: (p, j)) for p in range(n_parts)]
                + [pl.BlockSpec((bm, bn), lambda i, j: (i, j)), gain] + ([gain] if with_next else []))
    operands = list(mixes) + [w16] * n_parts + [x, g_post.reshape(1, d)]
    out_specs, out_shape = [row_block], [jax.ShapeDtypeStruct((m, d), F32)]
    if with_next:
        operands.append(g_next.reshape(1, d))
        out_specs.append(row_block)
        out_shape.append(jax.ShapeDtypeStruct((m, d), BF16))
    outs = pl.pallas_call(
        functools.partial(_out_proj_norm_kernel, n_parts=n_parts, bn=bn, with_next=with_next),
        grid=(m // bm, d // bn),
        in_specs=in_specs,
        out_specs=out_specs,
        out_shape=out_shape,
        scratch_shapes=[pltpu.VMEM((bm, d), F32)],
        compiler_params=_params(2),
        name=name,
    )(*operands)
    return (outs[0], outs[1]) if with_next else (outs[0], None)


def _matmul_kernel(*refs, n_parts, w_transposed):
    x_refs, w_refs = refs[:n_parts], refs[n_parts:2 * n_parts]
    o_ref = refs[2 * n_parts]
    w16_refs = refs[2 * n_parts + 1:]
    dot = _dot_nt if w_transposed else _dot

    @pl.when(pl.program_id(1) == 0)
    def _round_weights():
        for w_ref, w16_ref in zip(w_refs, w16_refs):
            w16_ref[...] = w_ref[...].astype(BF16)

    acc = dot(x_refs[0][...], w16_refs[0][...])
    for x_ref, w16_ref in zip(x_refs[1:], w16_refs[1:]):
        acc = acc + dot(x_ref[...], w16_ref[...])
    o_ref[...] = acc.astype(o_ref.dtype)


def _matmul(xs, w, out_dtype, *, bm, bn, n_col_blocks=None, w_transposed=False, name="matmul"):
    m = xs[0].shape[0]
    kp = xs[0].shape[1]
    n_parts = len(xs)
    k_axis, n_axis = (1, 0) if w_transposed else (0, 1)
    assert all(x.shape == (m, kp) for x in xs) and w.shape[k_axis] == n_parts * kp
    if n_col_blocks is None:
        n_col_blocks = w.shape[n_axis] // bn
    w_block = (bn, kp) if w_transposed else (kp, bn)

    def w_spec(p):
        return pl.BlockSpec(w_block, (lambda j, i: (j, p)) if w_transposed else (lambda j, i: (p, j)))

    return pl.pallas_call(
        functools.partial(_matmul_kernel, n_parts=n_parts, w_transposed=w_transposed),
        grid=(n_col_blocks, m // bm),
        in_specs=[pl.BlockSpec((bm, kp), lambda j, i: (i, 0)) for _ in xs]
                 + [w_spec(p) for p in range(n_parts)],
        out_specs=pl.BlockSpec((bm, bn), lambda j, i: (i, j)),
        out_shape=jax.ShapeDtypeStruct((m, n_col_blocks * bn), out_dtype),
        scratch_shapes=[pltpu.VMEM(w_block, BF16) for _ in xs],
        compiler_params=_params(2),
        name=name,
    )(*xs, *([w] * n_parts))


def _forget_proj_kernel(x_ref, wt_ref, af_ref, aft_ref):
    w16 = wt_ref[...].astype(BF16)
    x = x_ref[...]
    af_ref[...] = _dot_nt(x, w16)
    aft_ref[...] = _dot_nt(w16, x)


def _forget_proj(x, w_t, row0, n_rows, bm=1024):
    m, k = x.shape
    assert row0 % n_rows == 0 and w_t.shape[1] == k
    return pl.pallas_call(
        _forget_proj_kernel,
        grid=(m // bm,),
        in_specs=[pl.BlockSpec((bm, k), lambda i: (i, 0)),
                  pl.BlockSpec((n_rows, k), lambda i: (row0 // n_rows, 0))],
        out_specs=[pl.BlockSpec((bm, n_rows), lambda i: (i, 0)),
                   pl.BlockSpec((n_rows, bm), lambda i: (0, i))],
        out_shape=[jax.ShapeDtypeStruct((m, n_rows), F32),
                   jax.ShapeDtypeStruct((n_rows, m), F32)],
        compiler_params=_params(1),
        name="forget_proj",
    )(x, w_t)


def _forget_cumsum_kernel(af_ref, aft_ref, bcol_ref, brow_ref, ccol_ref, crow_ref, *, tc):
    s_len, nh = af_ref.shape
    r = lax.broadcasted_iota(jnp.int32, (tc, tc), 0)
    c = lax.broadcasted_iota(jnp.int32, (tc, tc), 1)
    lower = (c <= r).astype(F32).astype(BF16)
    upper = (r <= c).astype(F32).astype(BF16)

    def body(i, carry):
        carry_row, carry_col = carry
        t0 = pl.multiple_of(i * tc, tc)
        lf = _log_sigmoid(af_ref[pl.ds(t0, tc), :] + brow_ref[...])
        lft = _log_sigmoid(aft_ref[:, pl.ds(t0, tc)] + bcol_ref[...])
        a, b, d = _split3(lf)
        cc = (_dot(lower, a) + _dot(lower, b) + _dot(lower, d)) + carry_row
        a, b, d = _split3(lft)
        cr = (_dot(a, upper) + _dot(b, upper) + _dot(d, upper)) + carry_col
        ccol_ref[pl.ds(t0, tc), :] = cc * LOG2E
        crow_ref[:, pl.ds(t0, tc)] = cr * LOG2E
        return cc[tc - 1:tc, :], cr[:, tc - 1:tc]

    lax.fori_loop(0, s_len // tc, body,
                  (jnp.zeros((1, nh), F32), jnp.zeros((nh, 1), F32)))


def _forget_cumsum(af, aft, b_f, bsz, tc=256):
    m, nh = af.shape
    s_len = m // bsz
    return pl.pallas_call(
        functools.partial(_forget_cumsum_kernel, tc=tc),
        grid=(bsz,),
        in_specs=[pl.BlockSpec((s_len, nh), lambda b: (b, 0)),
                  pl.BlockSpec((nh, s_len), lambda b: (0, b)),
                  pl.BlockSpec((nh, 1), lambda b: (0, 0)),
                  pl.BlockSpec((1, nh), lambda b: (0, 0))],
        out_specs=[pl.BlockSpec((s_len, nh), lambda b: (b, 0)),
                   pl.BlockSpec((nh, s_len), lambda b: (0, b))],
        out_shape=[jax.ShapeDtypeStruct((m, nh), F32),
                   jax.ShapeDtypeStruct((nh, m), F32)],
        compiler_params=_params(1),
        name="forget_cumsum",
    )(af, aft, b_f.reshape(nh, 1), b_f.reshape(1, nh))


def _fox_kernel(q_ref, k_ref, v_ref, g_ref, crow_ref, ccol_ref, o_ref, *, tq, tk, scale2, hp):
    h0 = pl.program_id(1) * hp
    s_len = q_ref.shape[0]
    lane = lax.broadcasted_iota(jnp.int32, (tq, ccol_ref.shape[-1]), 1)
    row = lax.broadcasted_iota(jnp.int32, (tq, tk), 0)
    col = lax.broadcasted_iota(jnp.int32, (tq, tk), 1)
    heads = [slice(i * HEAD_DIM, (i + 1) * HEAD_DIM) for i in range(hp)]

    def q_body(qi, _):
        q0 = pl.multiple_of(qi * tq, tq)
        qs = [q_ref[pl.ds(q0, tq), hs] for hs in heads]
        c_blk = ccol_ref[pl.ds(q0, tq), :]
        c_ts = [jnp.sum(jnp.where(lane == h0 + i, c_blk, 0.0), axis=1, keepdims=True) for i in range(hp)]
        n_full = q0 // tk

        def logits(k0, width):
            return tuple(_dot_nt(qs[i], k_ref[pl.ds(k0, width), hs]) * scale2 - crow_ref[i, :, pl.ds(k0, width)]
                         for i, hs in enumerate(heads))

        def absorb(us, k0, width, state, diag_offset=None):
            if diag_offset is not None:
                keep = (lax.broadcasted_iota(jnp.int32, (tq, width), 1)
                        <= lax.broadcasted_iota(jnp.int32, (tq, width), 0) + diag_offset)
                us = [jnp.where(keep, u, NEG) for u in us]
            m_news = [jnp.maximum(st[0], jnp.max(u, axis=1, keepdims=True) + c_t)
                      for u, st, c_t in zip(us, state, c_ts)]
            ps = [jnp.exp2(u - (m_new - c_t)) for u, m_new, c_t in zip(us, m_news, c_ts)]
            pvs = [_dot(p.astype(BF16), v_ref[pl.ds(k0, width), hs]) for p, hs in zip(ps, heads)]
            out = []
            for (m, l, acc), m_new, p, pv in zip(state, m_news, ps, pvs):
                alpha = jnp.exp2(m - m_new)
                out.append((m_new, alpha * l + jnp.sum(p, axis=1, keepdims=True), alpha * acc + pv))
            return tuple(out)

        def step(kj, state):
            k0 = pl.multiple_of(kj * tk, tk)
            return absorb(logits(k0, tk), k0, tk, state)

        state = tuple((jnp.full((tq, 1), NEG, F32), jnp.zeros((tq, 1), F32), jnp.zeros((tq, HEAD_DIM), F32))
                      for _ in heads)
        state = lax.fori_loop(0, n_full, step, state)
        k_diag = pl.multiple_of(n_full * tk, tk)

        def diag_branch(r):
            width = (r + 1) * tq

            def run():
                gates = [_silu(g_ref[pl.ds(q0, tq), hs].astype(F32)) for hs in heads]
                final = absorb(logits(k_diag, width), k_diag, width, state, diag_offset=r * tq)
                for (_, l, acc), gate, hs in zip(final, gates, heads):
                    o_ref[pl.ds(q0, tq), hs] = ((acc / l) * gate).astype(o_ref.dtype)
                return 0

            return run

        return lax.switch(qi % (tk // tq), [diag_branch(r) for r in range(tk // tq)])

    lax.fori_loop(0, s_len // tq, q_body, 0)


def _fox_attention(proj, c_row, c_col, bsz, s_len, n_heads, col_blocks, tq=256, tk=1024, hp=2):
    qb, kb, vb, gb = col_blocks
    scale2 = LOG2E * HEAD_DIM ** -0.5
    assert n_heads % hp == 0 and all(off % hp == 0 for off in col_blocks)
    assert tk % tq == 0 and s_len % tk == 0

    def head_spec(off):
        return pl.BlockSpec((s_len, hp * HEAD_DIM), lambda b, h: (b, off // hp + h))

    return pl.pallas_call(
        functools.partial(_fox_kernel, tq=tq, tk=tk, scale2=scale2, hp=hp),
        grid=(bsz, n_heads // hp),
        in_specs=[head_spec(qb), head_spec(kb), head_spec(vb), head_spec(gb),
                  pl.BlockSpec((hp, 1, s_len), lambda b, h: (h, 0, b)),
                  pl.BlockSpec((s_len, n_heads), lambda b, h: (b, 0))],
        out_specs=pl.BlockSpec((s_len, hp * HEAD_DIM), lambda b, h: (b, h)),
        out_shape=jax.ShapeDtypeStruct((bsz * s_len, n_heads * HEAD_DIM), BF16),
        compiler_params=_params(2),
        name="fox_attention",
    )(proj, proj, proj, proj, c_row.reshape(n_heads, 1, bsz * s_len), c_col)


def _chunk_kernel(q_ref, k_ref, v_ref, g_ref, rel_ref, o_ref, kpad, vpad, bias_ref, *, tq, scale, hp):
    b = pl.program_id(1)
    s_len = q_ref.shape[0]
    pad = LEFT_CHUNKS * CHUNK
    win = tq + pad
    rp = rel_ref.shape[-1]
    heads = [slice(i * HEAD_DIM, (i + 1) * HEAD_DIM) for i in range(hp)]

    @pl.when(b == 0)
    def _build_bias():
        u = lax.broadcasted_iota(jnp.int32, (rp, win), 1)
        r = lax.broadcasted_iota(jnp.int32, (rp, win), 0)
        idx = jnp.where(u < pad + CHUNK, jnp.clip(pad - u, -REL_CLIP, REL_CLIP) + REL_CLIP, 2 * REL_CLIP)
        sel = (r == idx).astype(F32).astype(BF16)
        qc = lax.broadcasted_iota(jnp.int32, (tq, win), 0) // CHUNK
        kc = lax.broadcasted_iota(jnp.int32, (tq, win), 1) // CHUNK
        in_band = (kc >= qc) & (kc <= qc + LEFT_CHUNKS)
        for i in range(hp):
            a, bb, d = _split3(jnp.broadcast_to(rel_ref[i], (8, rp)))
            g = _dot(a, sel) + _dot(bb, sel) + _dot(d, sel)
            tile = pltpu.roll(jnp.broadcast_to(g[0:1, :], (tq, win)), 0, 1, stride=1, stride_axis=0)
            bias_ref[i] = jnp.where(in_band, tile, NEG)

    kpad[pl.ds(0, pad), :] = jnp.zeros((pad, hp * HEAD_DIM), kpad.dtype)
    vpad[pl.ds(0, pad), :] = jnp.zeros((pad, hp * HEAD_DIM), vpad.dtype)
    kpad[pl.ds(pad, s_len), :] = k_ref[...]
    vpad[pl.ds(pad, s_len), :] = v_ref[...]
    key_pos = lax.broadcasted_iota(jnp.int32, (tq, win), 1)

    def body(j, _, before_start):
        q0 = pl.multiple_of(j * tq, tq)
        ss = [_dot_nt(q_ref[pl.ds(q0, tq), hs], kpad[pl.ds(q0, win), hs]) * scale + bias_ref[i]
              for i, hs in enumerate(heads)]
        if before_start:
            ss = [jnp.where(key_pos + q0 >= pad, s, NEG) for s in ss]
        ps = [jnp.exp(s - jnp.max(s, axis=1, keepdims=True)) for s in ss]
        pvs = [_dot(p.astype(BF16), vpad[pl.ds(q0, win), hs]) for p, hs in zip(ps, heads)]
        for p, pv, hs in zip(ps, pvs, heads):
            o = pv / jnp.sum(p, axis=1, keepdims=True)
            g = g_ref[pl.ds(q0, tq), hs].astype(F32)
            o_ref[pl.ds(q0, tq), hs] = (o * _silu(g)).astype(o_ref.dtype)
        return 0

    n_start = pad // tq
    lax.fori_loop(0, n_start, functools.partial(body, before_start=True), 0)
    lax.fori_loop(n_start, s_len // tq, functools.partial(body, before_start=False), 0)


def _chunk_attention(proj, rel_bias, bsz, s_len, n_heads, col_blocks, tq=256, hp=2):
    qb, kb, vb, gb = col_blocks
    scale = HEAD_DIM ** -0.5
    pad = LEFT_CHUNKS * CHUNK
    rp = 3 * LANES
    assert N_REL <= rp and pad % tq == 0
    assert n_heads % hp == 0 and all(off % hp == 0 for off in col_blocks)
    rel = jnp.pad(rel_bias, ((0, 0), (0, rp - N_REL))).reshape(n_heads, 1, rp)

    def head_spec(off):
        return pl.BlockSpec((s_len, hp * HEAD_DIM), lambda h, b: (b, off // hp + h))

    return pl.pallas_call(
        functools.partial(_chunk_kernel, tq=tq, scale=scale, hp=hp),
        grid=(n_heads // hp, bsz),
        in_specs=[head_spec(qb), head_spec(kb), head_spec(vb), head_spec(gb),
                  pl.BlockSpec((hp, 1, rp), lambda h, b: (h, 0, 0))],
        out_specs=pl.BlockSpec((s_len, hp * HEAD_DIM), lambda h, b: (b, h)),
        out_shape=jax.ShapeDtypeStruct((bsz * s_len, n_heads * HEAD_DIM), BF16),
        scratch_shapes=[pltpu.VMEM((s_len + pad, hp * HEAD_DIM), BF16),
                        pltpu.VMEM((s_len + pad, hp * HEAD_DIM), BF16),
                        pltpu.VMEM((hp, tq, tq + pad), F32)],
        compiler_params=_params(2),
        name="chunk_attention",
    )(proj, proj, proj, proj, rel)


def _sb_kernel(q_ref, k_ref, v_ref, g_ref, o_ref, *, tq, scale2, hp):
    s_len = q_ref.shape[0]
    row = lax.broadcasted_iota(jnp.int32, (tq, tq), 0)
    col = lax.broadcasted_iota(jnp.int32, (tq, tq), 1)
    strict = col < row
    after = (row > col).astype(F32).astype(BF16)
    after2 = jnp.concatenate([after, after], axis=0)
    heads = [slice(i * HEAD_DIM, (i + 1) * HEAD_DIM) for i in range(hp)]

    def log2_terms(q, k):
        z = _dot_nt(q, k) * scale2
        lb = jnp.minimum(z, 0.0) - jnp.log2(1.0 + jnp.exp2(-jnp.abs(z)))
        return lb, lb - z

    def suffix_sums(lom):
        hi = lom.astype(BF16)
        lo = (lom - hi.astype(F32)).astype(BF16)
        tail = _dot(jnp.concatenate([hi, lo], axis=1), after2)
        return tail, tail[:, 0:1] + lom[:, 0:1]

    def finish(q0, hs, acc):
        g = g_ref[pl.ds(q0, tq), hs].astype(F32)
        o_ref[pl.ds(q0, tq), hs] = (acc * _silu(g)).astype(o_ref.dtype)

    def diag_tile(q, q0, hs):
        lb, lom = log2_terms(q, k_ref[pl.ds(q0, tq), hs])
        lom = jnp.where(strict, lom, 0.0)
        tail, rsum = suffix_sums(lom)
        a = jnp.where(strict, jnp.exp2(lb + tail), 0.0)
        return a, rsum

    for hs in heads:
        a, _ = diag_tile(q_ref[pl.ds(0, tq), hs], 0, hs)
        finish(0, hs, _dot(a.astype(BF16), v_ref[pl.ds(0, tq), hs]))

    def more(carry):
        j, rsum, _ = carry
        return jnp.logical_and(j >= 0, jnp.max(rsum) >= SB_STOP_LOG2)

    def q_body(qi, _):
        q0 = pl.multiple_of(qi * tq, tq)
        k1 = pl.multiple_of(q0 - tq, tq)
        qs = [q_ref[pl.ds(q0, tq), hs] for hs in heads]
        terms = [log2_terms(q, k_ref[pl.ds(k1, 2 * tq), hs]) for q, hs in zip(qs, heads)]
        sums = []
        for lb, lom in terms:
            tail_d, rsum_d = suffix_sums(jnp.where(strict, lom[:, tq:], 0.0))
            tail_l, rsum_l = suffix_sums(lom[:, :tq])
            sums.append((tail_l, tail_d, rsum_d, rsum_d + rsum_l))
        accs = []
        for (lb, _), (tail_l, tail_d, rsum_d, _), hs in zip(terms, sums, heads):
            a_l = jnp.exp2(lb[:, :tq] + tail_l + rsum_d)
            a_d = jnp.where(strict, jnp.exp2(lb[:, tq:] + tail_d), 0.0)
            a = jnp.concatenate([a_l, a_d], axis=1).astype(BF16)
            accs.append(_dot(a, v_ref[pl.ds(k1, 2 * tq), hs]))
        gates = [_silu(g_ref[pl.ds(q0, tq), hs].astype(F32)) for hs in heads]
        for hs, acc, gate in zip(heads, accs, gates):
            o_ref[pl.ds(q0, tq), hs] = (acc * gate).astype(o_ref.dtype)

        worst = functools.reduce(jnp.maximum, [s[3] for s in sums])

        @pl.when(jnp.logical_and(qi >= 2, jnp.max(worst) >= SB_STOP_LOG2))
        def _visit_more_tiles():
            for hs, q, (_, _, _, rsum), acc, gate in zip(heads, qs, sums, accs, gates):
                def tile(carry, q=q, hs=hs):
                    j, rsum, acc = carry
                    k0 = pl.multiple_of(j * tq, tq)
                    lb, lom = log2_terms(q, k_ref[pl.ds(k0, tq), hs])
                    tail, rs = suffix_sums(lom)
                    a = jnp.exp2(lb + tail + rsum)
                    acc = acc + _dot(a.astype(BF16), v_ref[pl.ds(k0, tq), hs])
                    return j - 1, rsum + rs, acc

                acc = lax.while_loop(more, tile, (qi - 2, rsum, acc))[2]
                o_ref[pl.ds(q0, tq), hs] = (acc * gate).astype(o_ref.dtype)
        return 0

    lax.fori_loop(1, s_len // tq, q_body, 0)


def _sb_attention(proj, bsz, s_len, n_heads, col_blocks, tq=256, hp=4):
    qb, kb, vb, gb = col_blocks
    scale2 = LOG2E * HEAD_DIM ** -0.5
    assert n_heads % hp == 0 and all(off % hp == 0 for off in col_blocks)

    def head_spec(off):
        return pl.BlockSpec((s_len, hp * HEAD_DIM), lambda b, h: (b, off // hp + h))

    return pl.pallas_call(
        functools.partial(_sb_kernel, tq=tq, scale2=scale2, hp=hp),
        grid=(bsz, n_heads // hp),
        in_specs=[head_spec(qb), head_spec(kb), head_spec(vb), head_spec(gb)],
        out_specs=pl.BlockSpec((s_len, hp * HEAD_DIM), lambda b, h: (b, h)),
        out_shape=jax.ShapeDtypeStruct((bsz * s_len, n_heads * HEAD_DIM), BF16),
        compiler_params=_params(2),
        name="sb_attention",
    )(proj, proj, proj, proj)


def _even_layer(x2, h, bsz, s_len, g_post, g_next, w_in, b_f, rel_bias, w_out):
    n_a = b_f.shape[0]
    n_b = rel_bias.shape[0]
    main_cols = 4 * (n_a + n_b) * HEAD_DIM
    w_in_t = w_in.T
    proj = _matmul([h], w_in_t, BF16, bm=PROJ_BM, bn=PROJ_BN, n_col_blocks=main_cols // PROJ_BN,
                   w_transposed=True, name="in_proj_even")
    af, aft = _forget_proj(h, w_in_t, main_cols, n_a)
    c_col, c_row = _forget_cumsum(af, aft, b_f, bsz)
    mix_a = _fox_attention(proj, c_row, c_col, bsz, s_len, n_a,
                           (0, n_a, 2 * n_a, 3 * n_a))
    off = 4 * n_a
    mix_b = _chunk_attention(proj, rel_bias, bsz, s_len, n_b,
                             (off, off + n_b, off + 2 * n_b, off + 3 * n_b))
    return _out_proj_norm([mix_a, mix_b], w_out.astype(BF16), x2, g_post, g_next,
                          bm=OUT_BM, bn=OUT_BN, name="out_proj_norm_even")


def _odd_layer(x2, h, bsz, s_len, g_post, g_next, w_in, w_out):
    n_c = w_out.shape[0] // HEAD_DIM
    proj = _matmul([h], w_in, BF16, bm=PROJ_BM, bn=PROJ_BN, name="in_proj_odd")
    mix = _sb_attention(proj, bsz, s_len, n_c, (0, n_c, 2 * n_c, 3 * n_c))
    return _out_proj_norm([mix], w_out.astype(BF16), x2, g_post, g_next,
                          bm=OUT_BM, bn=OUT_BN, name="out_proj_norm_odd")


def kernel(x, norm_pre, norm_post, w_in_even, b_f_even, rel_bias_even, w_out_even, w_in_odd, w_out_odd):
    bsz, s_len, d = x.shape
    depth = norm_pre.shape[0]
    x2 = x.reshape(bsz * s_len, d)
    h = _rmsnorm(x2, norm_pre[0], BF16)
    for layer in range(depth):
        i = layer // 2
        g_next = norm_pre[layer + 1] if layer + 1 < depth else None
        if layer % 2 == 0:
            x2, h = _even_layer(x2, h, bsz, s_len, norm_post[layer], g_next,
                                w_in_even[i], b_f_even[i], rel_bias_even[i], w_out_even[i])
        else:
            x2, h = _odd_layer(x2, h, bsz, s_len, norm_post[layer], g_next,
                               w_in_odd[i], w_out_odd[i])
    return x2.reshape(bsz, s_len, d)
```

```python
import functools

import jax
import jax.numpy as jnp
from jax import lax
from jax.experimental import pallas as pl
from jax.experimental.pallas import tpu as pltpu

HEAD_DIM = 128
CHUNK = 64
LEFT_CHUNKS = 8
REL_CLIP = 128
N_REL = 2 * REL_CLIP + 1
RMS_EPS = 1e-6

LANES = 128
VMEM_LIMIT = 60 * 1024 * 1024
PROJ_BM, PROJ_BN = 512, 1024
OUT_BM, OUT_BN = 512, 512
NORM_ROWS = 128
NEG = -1e30
LOG2E = 1.4426950408889634
SB_STOP_LOG2 = -152.0

BF16 = jnp.bfloat16
F32 = jnp.float32


def _params(n_axes):
    return pltpu.CompilerParams(dimension_semantics=("arbitrary",) * n_axes,
                                vmem_limit_bytes=VMEM_LIMIT)


def _dot_nt(a, b):
    return lax.dot_general(a, b, (((1,), (1,)), ((), ())), preferred_element_type=F32)


def _dot(a, b):
    return jnp.dot(a, b, preferred_element_type=F32)


def _split3(x):
    hi = x.astype(BF16)
    r1 = x - hi.astype(F32)
    mid = r1.astype(BF16)
    lo = (r1 - mid.astype(F32)).astype(BF16)
    return hi, mid, lo


def _log_sigmoid(z):
    return jnp.minimum(z, 0.0) - jnp.log1p(jnp.exp(-jnp.abs(z)))


def _silu(g):
    return g / (1.0 + jnp.exp(-g))


def _rmsnorm_kernel(x_ref, g_ref, o_ref):
    x = x_ref[...]
    ms = jnp.mean(x * x, axis=-1, keepdims=True)
    o_ref[...] = (x * lax.rsqrt(ms + RMS_EPS) * g_ref[...]).astype(o_ref.dtype)


def _rmsnorm(x, g, out_dtype, bm=256):
    m, d = x.shape
    return pl.pallas_call(
        _rmsnorm_kernel,
        grid=(m // bm,),
        in_specs=[pl.BlockSpec((bm, d), lambda i: (i, 0)),
                  pl.BlockSpec((1, d), lambda i: (0, 0))],
        out_specs=pl.BlockSpec((bm, d), lambda i: (i, 0)),
        out_shape=jax.ShapeDtypeStruct((m, d), out_dtype),
        compiler_params=_params(1),
        name="rmsnorm",
    )(x, g.reshape(1, d))


def _out_proj_norm_kernel(*refs, n_parts, bn, with_next):
    mix_refs, w_refs = refs[:n_parts], refs[n_parts:2 * n_parts]
    rest = list(refs[2 * n_parts:])
    xres_ref, gpost_ref = rest.pop(0), rest.pop(0)
    gnext_ref = rest.pop(0) if with_next else None
    o_ref = rest.pop(0)
    h_ref = rest.pop(0) if with_next else None
    xs_ref = rest.pop(0)
    j = pl.program_id(1)
    c0 = pl.multiple_of(j * bn, bn)

    y = _dot(mix_refs[0][...], w_refs[0][...])
    for mix_ref, w_ref in zip(mix_refs[1:], w_refs[1:]):
        y = y + _dot(mix_ref[...], w_ref[...])
    o_ref[:, pl.ds(c0, bn)] = y
    xs_ref[:, pl.ds(c0, bn)] = xres_ref[...]

    @pl.when(j == pl.num_programs(1) - 1)
    def _normalise():
        def rows(r, _):
            rs = pl.ds(pl.multiple_of(r * NORM_ROWS, NORM_ROWS), NORM_ROWS)
            y = o_ref[rs, :]
            ms = jnp.mean(y * y, axis=-1, keepdims=True)
            x_new = xs_ref[rs, :] + y * lax.rsqrt(ms + RMS_EPS) * gpost_ref[...]
            o_ref[rs, :] = x_new
            if with_next:
                ms = jnp.mean(x_new * x_new, axis=-1, keepdims=True)
                h_ref[rs, :] = (x_new * lax.rsqrt(ms + RMS_EPS) * gnext_ref[...]).astype(h_ref.dtype)
            return 0

        lax.fori_loop(0, o_ref.shape[0] // NORM_ROWS, rows, 0)


def _out_proj_norm(mixes, w16, x, g_post, g_next, *, bm, bn, name):
    m, d = x.shape
    kp = mixes[0].shape[1]
    n_parts = len(mixes)
    with_next = g_next is not None
    assert all(t.shape == (m, kp) for t in mixes) and w16.shape == (n_parts * kp, d)
    row_block = pl.BlockSpec((bm, d), lambda i, j: (i, 0))
    gain = pl.BlockSpec((1, d), lambda i, j: (0, 0))
    in_specs = ([pl.BlockSpec((bm, kp), lambda i, j: (i, 0)) for _ in mixes]
                + [pl.BlockSpec((kp, bn), lambda i, j, p=p: (p, j)) for p in range(n_parts)]
                + [pl.BlockSpec((bm, bn), lambda i, j: (i, j)), gain] + ([gain] if with_next else []))
    operands = list(mixes) + [w16] * n_parts + [x, g_post.reshape(1, d)]
    out_specs, out_shape = [row_block], [jax.ShapeDtypeStruct((m, d), F32)]
    if with_next:
        operands.append(g_next.reshape(1, d))
        out_specs.append(row_block)
        out_shape.append(jax.ShapeDtypeStruct((m, d), BF16))
    outs = pl.pallas_call(
        functools.partial(_out_proj_norm_kernel, n_parts=n_parts, bn=bn, with_next=with_next),
        grid=(m // bm, d // bn),
        in_specs=in_specs,
        out_specs=out_specs,
        out_shape=out_shape,
        scratch_shapes=[pltpu.VMEM((bm, d), F32)],
        compiler_params=_params(2),
        name=name,
    )(*operands)
    return (outs[0], outs[1]) if with_next else (outs[0], None)


def _matmul_kernel(*refs, n_parts, w_transposed):
    x_refs, w_refs = refs[:n_parts], refs[n_parts:2 * n_parts]
    o_ref = refs[2 * n_parts]
    w16_refs = refs[2 * n_parts + 1:]
    dot = _dot_nt if w_transposed else _dot

    @pl.when(pl.program_id(1) == 0)
    def _round_weights():
        for w_ref, w16_ref in zip(w_refs, w16_refs):
            w16_ref[...] = w_ref[...].astype(BF16)

    acc = dot(x_refs[0][...], w16_refs[0][...])
    for x_ref, w16_ref in zip(x_refs[1:], w16_refs[1:]):
        acc = acc + dot(x_ref[...], w16_ref[...])
    o_ref[...] = acc.astype(o_ref.dtype)


def _matmul(xs, w, out_dtype, *, bm, bn, n_col_blocks=None, w_transposed=False, name="matmul"):
    m = xs[0].shape[0]
    kp = xs[0].shape[1]
    n_parts = len(xs)
    k_axis, n_axis = (1, 0) if w_transposed else (0, 1)
    assert all(x.shape == (m, kp) for x in xs) and w.shape[k_axis] == n_parts * kp
    if n_col_blocks is None:
        n_col_blocks = w.shape[n_axis] // bn
    w_block = (bn, kp) if w_transposed else (kp, bn)

    def w_spec(p):
        return pl.BlockSpec(w_block, (lambda j, i: (j, p)) if w_transposed else (lambda j, i: (p, j)))

    return pl.pallas_call(
        functools.partial(_matmul_kernel, n_parts=n_parts, w_transposed=w_transposed),
        grid=(n_col_blocks, m // bm),
        in_specs=[pl.BlockSpec((bm, kp), lambda j, i: (i, 0)) for _ in xs]
                 + [w_spec(p) for p in range(n_parts)],
        out_specs=pl.BlockSpec((bm, bn), lambda j, i: (i, j)),
        out_shape=jax.ShapeDtypeStruct((m, n_col_blocks * bn), out_dtype),
        scratch_shapes=[pltpu.VMEM(w_block, BF16) for _ in xs],
        compiler_params=_params(2),
        name=name,
    )(*xs, *([w] * n_parts))


def _forget_proj_kernel(x_ref, wt_ref, af_ref, aft_ref):
    w16 = wt_ref[...].astype(BF16)
    x = x_ref[...]
    af_ref[...] = _dot_nt(x, w16)
    aft_ref[...] = _dot_nt(w16, x)


def _forget_proj(x, w_t, row0, n_rows, bm=1024):
    m, k = x.shape
    assert row0 % n_rows == 0 and w_t.shape[1] == k
    return pl.pallas_call(
        _forget_proj_kernel,
        grid=(m // bm,),
        in_specs=[pl.BlockSpec((bm, k), lambda i: (i, 0)),
                  pl.BlockSpec((n_rows, k), lambda i: (row0 // n_rows, 0))],
        out_specs=[pl.BlockSpec((bm, n_rows), lambda i: (i, 0)),
                   pl.BlockSpec((n_rows, bm), lambda i: (0, i))],
        out_shape=[jax.ShapeDtypeStruct((m, n_rows), F32),
                   jax.ShapeDtypeStruct((n_rows, m), F32)],
        compiler_params=_params(1),
        name="forget_proj",
    )(x, w_t)


def _forget_cumsum_kernel(af_ref, aft_ref, bcol_ref, brow_ref, ccol_ref, crow_ref, *, tc):
    s_len, nh = af_ref.shape
    r = lax.broadcasted_iota(jnp.int32, (tc, tc), 0)
    c = lax.broadcasted_iota(jnp.int32, (tc, tc), 1)
    lower = (c <= r).astype(F32).astype(BF16)
    upper = (r <= c).astype(F32).astype(BF16)

    def body(i, carry):
        carry_row, carry_col = carry
        t0 = pl.multiple_of(i * tc, tc)
        lf = _log_sigmoid(af_ref[pl.ds(t0, tc), :] + brow_ref[...])
        lft = _log_sigmoid(aft_ref[:, pl.ds(t0, tc)] + bcol_ref[...])
        a, b, d = _split3(lf)
        cc = (_dot(lower, a) + _dot(lower, b) + _dot(lower, d)) + carry_row
        a, b, d = _split3(lft)
        cr = (_dot(a, upper) + _dot(b, upper) + _dot(d, upper)) + carry_col
        ccol_ref[pl.ds(t0, tc), :] = cc * LOG2E
        crow_ref[:, pl.ds(t0, tc)] = cr * LOG2E
        return cc[tc - 1:tc, :], cr[:, tc - 1:tc]

    lax.fori_loop(0, s_len // tc, body,
                  (jnp.zeros((1, nh), F32), jnp.zeros((nh, 1), F32)))


def _forget_cumsum(af, aft, b_f, bsz, tc=256):
    m, nh = af.shape
    s_len = m // bsz
    return pl.pallas_call(
        functools.partial(_forget_cumsum_kernel, tc=tc),
        grid=(bsz,),
        in_specs=[pl.BlockSpec((s_len, nh), lambda b: (b, 0)),
                  pl.BlockSpec((nh, s_len), lambda b: (0, b)),
                  pl.BlockSpec((nh, 1), lambda b: (0, 0)),
                  pl.BlockSpec((1, nh), lambda b: (0, 0))],
        out_specs=[pl.BlockSpec((s_len, nh), lambda b: (b, 0)),
                   pl.BlockSpec((nh, s_len), lambda b: (0, b))],
        out_shape=[jax.ShapeDtypeStruct((m, nh), F32),
                   jax.ShapeDtypeStruct((nh, m), F32)],
        compiler_params=_params(1),
        name="forget_cumsum",
    )(af, aft, b_f.reshape(nh, 1), b_f.reshape(1, nh))


def _fox_kernel(q_ref, k_ref, v_ref, g_ref, crow_ref, ccol_ref, o_ref, *, tq, tk, scale2, hp):
    h0 = pl.program_id(1) * hp
    s_len = q_ref.shape[0]
    lane = lax.broadcasted_iota(jnp.int32, (tq, ccol_ref.shape[-1]), 1)
    row = lax.broadcasted_iota(jnp.int32, (tq, tk), 0)
    col = lax.broadcasted_iota(jnp.int32, (tq, tk), 1)
    heads = [slice(i * HEAD_DIM, (i + 1) * HEAD_DIM) for i in range(hp)]

    def q_body(qi, _):
        q0 = pl.multiple_of(qi * tq, tq)
        qs = [q_ref[pl.ds(q0, tq), hs] for hs in heads]
        c_blk = ccol_ref[pl.ds(q0, tq), :]
        c_ts = [jnp.sum(jnp.where(lane == h0 + i, c_blk, 0.0), axis=1, keepdims=True) for i in range(hp)]
        n_full = q0 // tk

        def logits(k0, width):
            return tuple(_dot_nt(qs[i], k_ref[pl.ds(k0, width), hs]) * scale2 - crow_ref[i, :, pl.ds(k0, width)]
                         for i, hs in enumerate(heads))

        def absorb(us, k0, width, state, diag_offset=None):
            if diag_offset is not None:
                keep = (lax.broadcasted_iota(jnp.int32, (tq, width), 1)
                        <= lax.broadcasted_iota(jnp.int32, (tq, width), 0) + diag_offset)
                us = [jnp.where(keep, u, NEG) for u in us]
            m_news = [jnp.maximum(st[0], jnp.max(u, axis=1, keepdims=True) + c_t)
                      for u, st, c_t in zip(us, state, c_ts)]
            ps = [jnp.exp2(u - (m_new - c_t)) for u, m_new, c_t in zip(us, m_news, c_ts)]
            pvs = [_dot(p.astype(BF16), v_ref[pl.ds(k0, width), hs]) for p, hs in zip(ps, heads)]
            out = []
            for (m, l, acc), m_new, p, pv in zip(state, m_news, ps, pvs):
                alpha = jnp.exp2(m - m_new)
                out.append((m_new, alpha * l + jnp.sum(p, axis=1, keepdims=True), alpha * acc + pv))
            return tuple(out)

        def absorb_tiles(tiles, state):
            all_us = [logits(k0, width) for k0, width, _ in tiles]
            for us, (k0, width, diag_offset) in zip(all_us, tiles):
                state = absorb(us, k0, width, state, diag_offset)
            return state

        def step2(kp, state):
            k0 = pl.multiple_of(2 * kp * tk, tk)
            return absorb_tiles([(k0, tk, None), (k0 + tk, tk, None)], state)

        state = tuple((jnp.full((tq, 1), NEG, F32), jnp.zeros((tq, 1), F32), jnp.zeros((tq, HEAD_DIM), F32))
                      for _ in heads)
        state = lax.fori_loop(0, n_full // 2, step2, state)
        k_diag = pl.multiple_of(n_full * tk, tk)
        n_r = tk // tq

        def last_branch(odd, r):
            width = (r + 1) * tq

            def run():
                gates = [_silu(g_ref[pl.ds(q0, tq), hs].astype(F32)) for hs in heads]
                tiles = [(pl.multiple_of(k_diag - tk, tk), tk, None)] if odd else []
                final = absorb_tiles(tiles + [(k_diag, width, r * tq)], state)
                for (_, l, acc), gate, hs in zip(final, gates, heads):
                    o_ref[pl.ds(q0, tq), hs] = ((acc / l) * gate).astype(o_ref.dtype)
                return 0

            return run

        return lax.switch((n_full % 2) * n_r + qi % n_r,
                          [last_branch(odd, r) for odd in (0, 1) for r in range(n_r)])

    lax.fori_loop(0, s_len // tq, q_body, 0)


def _fox_attention(proj, c_row, c_col, bsz, s_len, n_heads, col_blocks, tq=256, tk=1024, hp=2):
    qb, kb, vb, gb = col_blocks
    scale2 = LOG2E * HEAD_DIM ** -0.5
    assert n_heads % hp == 0 and all(off % hp == 0 for off in col_blocks)
    assert tk % tq == 0 and s_len % tk == 0

    def head_spec(off):
        return pl.BlockSpec((s_len, hp * HEAD_DIM), lambda b, h: (b, off // hp + h))

    return pl.pallas_call(
        functools.partial(_fox_kernel, tq=tq, tk=tk, scale2=scale2, hp=hp),
        grid=(bsz, n_heads // hp),
        in_specs=[head_spec(qb), head_spec(kb), head_spec(vb), head_spec(gb),
                  pl.BlockSpec((hp, 1, s_len), lambda b, h: (h, 0, b)),
                  pl.BlockSpec((s_len, n_heads), lambda b, h: (b, 0))],
        out_specs=pl.BlockSpec((s_len, hp * HEAD_DIM), lambda b, h: (b, h)),
        out_shape=jax.ShapeDtypeStruct((bsz * s_len, n_heads * HEAD_DIM), BF16),
        compiler_params=_params(2),
        name="fox_attention",
    )(proj, proj, proj, proj, c_row.reshape(n_heads, 1, bsz * s_len), c_col)


def _chunk_kernel(q_ref, k_ref, v_ref, g_ref, rel_ref, o_ref, kpad, vpad, bias_ref, *, tq, scale, hp):
    b = pl.program_id(1)
    s_len = q_ref.shape[0]
    pad = LEFT_CHUNKS * CHUNK
    win = tq + pad
    rp = rel_ref.shape[-1]
    heads = [slice(i * HEAD_DIM, (i + 1) * HEAD_DIM) for i in range(hp)]

    @pl.when(b == 0)
    def _build_bias():
        u = lax.broadcasted_iota(jnp.int32, (rp, win), 1)
        r = lax.broadcasted_iota(jnp.int32, (rp, win), 0)
        idx = jnp.where(u < pad + CHUNK, jnp.clip(pad - u, -REL_CLIP, REL_CLIP) + REL_CLIP, 2 * REL_CLIP)
        sel = (r == idx).astype(F32).astype(BF16)
        qc = lax.broadcasted_iota(jnp.int32, (tq, win), 0) // CHUNK
        kc = lax.broadcasted_iota(jnp.int32, (tq, win), 1) // CHUNK
        in_band = (kc >= qc) & (kc <= qc + LEFT_CHUNKS)
        for i in range(hp):
            a, bb, d = _split3(jnp.broadcast_to(rel_ref[i], (8, rp)))
            g = _dot(a, sel) + _dot(bb, sel) + _dot(d, sel)
            tile = pltpu.roll(jnp.broadcast_to(g[0:1, :], (tq, win)), 0, 1, stride=1, stride_axis=0)
            bias_ref[i] = jnp.where(in_band, tile, NEG)

    kpad[pl.ds(0, pad), :] = jnp.zeros((pad, hp * HEAD_DIM), kpad.dtype)
    vpad[pl.ds(0, pad), :] = jnp.zeros((pad, hp * HEAD_DIM), vpad.dtype)
    kpad[pl.ds(pad, s_len), :] = k_ref[...]
    vpad[pl.ds(pad, s_len), :] = v_ref[...]
    key_pos = lax.broadcasted_iota(jnp.int32, (tq, win), 1)

    def body(j, _, before_start):
        q0 = pl.multiple_of(j * tq, tq)
        ss = [_dot_nt(q_ref[pl.ds(q0, tq), hs], kpad[pl.ds(q0, win), hs]) * scale + bias_ref[i]
              for i, hs in enumerate(heads)]
        if before_start:
            ss = [jnp.where(key_pos + q0 >= pad, s, NEG) for s in ss]
        ps = [jnp.exp(s - jnp.max(s, axis=1, keepdims=True)) for s in ss]
        pvs = [_dot(p.astype(BF16), vpad[pl.ds(q0, win), hs]) for p, hs in zip(ps, heads)]
        for p, pv, hs in zip(ps, pvs, heads):
            o = pv / jnp.sum(p, axis=1, keepdims=True)
            g = g_ref[pl.ds(q0, tq), hs].astype(F32)
            o_ref[pl.ds(q0, tq), hs] = (o * _silu(g)).astype(o_ref.dtype)
        return 0

    n_start = pad // tq
    lax.fori_loop(0, n_start, functools.partial(body, before_start=True), 0)
    lax.fori_loop(n_start, s_len // tq, functools.partial(body, before_start=False), 0)


def _chunk_attention(proj, rel_bias, bsz, s_len, n_heads, col_blocks, tq=256, hp=4):
    qb, kb, vb, gb = col_blocks
    scale = HEAD_DIM ** -0.5
    pad = LEFT_CHUNKS * CHUNK
    rp = 3 * LANES
    assert N_REL <= rp and pad % tq == 0
    assert n_heads % hp == 0 and all(off % hp == 0 for off in col_blocks)
    rel = jnp.pad(rel_bias, ((0, 0), (0, rp - N_REL))).reshape(n_heads, 1, rp)

    def head_spec(off):
        return pl.BlockSpec((s_len, hp * HEAD_DIM), lambda h, b: (b, off // hp + h))

    return pl.pallas_call(
        functools.partial(_chunk_kernel, tq=tq, scale=scale, hp=hp),
        grid=(n_heads // hp, bsz),
        in_specs=[head_spec(qb), head_spec(kb), head_spec(vb), head_spec(gb),
                  pl.BlockSpec((hp, 1, rp), lambda h, b: (h, 0, 0))],
        out_specs=pl.BlockSpec((s_len, hp * HEAD_DIM), lambda h, b: (b, h)),
        out_shape=jax.ShapeDtypeStruct((bsz * s_len, n_heads * HEAD_DIM), BF16),
        scratch_shapes=[pltpu.VMEM((s_len + pad, hp * HEAD_DIM), BF16),
                        pltpu.VMEM((s_len + pad, hp * HEAD_DIM), BF16),
                        pltpu.VMEM((hp, tq, tq + pad), F32)],
        compiler_params=_params(2),
        name="chunk_attention",
    )(proj, proj, proj, proj, rel)


def _sb_kernel(q_ref, k_ref, v_ref, g_ref, o_ref, *, tq, scale2, hp):
    s_len = q_ref.shape[0]
    row = lax.broadcasted_iota(jnp.int32, (tq, tq), 0)
    col = lax.broadcasted_iota(jnp.int32, (tq, tq), 1)
    strict = col < row
    after = (row > col).astype(F32).astype(BF16)
    after2 = jnp.concatenate([after, after], axis=0)
    heads = [slice(i * HEAD_DIM, (i + 1) * HEAD_DIM) for i in range(hp)]

    def log2_terms(q, k):
        z = _dot_nt(q, k) * scale2
        lb = jnp.minimum(z, 0.0) - jnp.log2(1.0 + jnp.exp2(-jnp.abs(z)))
        return lb, lb - z

    def suffix_sums(lom):
        hi = lom.astype(BF16)
        lo = (lom - hi.astype(F32)).astype(BF16)
        tail = _dot(jnp.concatenate([hi, lo], axis=1), after2)
        return tail, tail[:, 0:1] + lom[:, 0:1]

    def finish(q0, hs, acc):
        g = g_ref[pl.ds(q0, tq), hs].astype(F32)
        o_ref[pl.ds(q0, tq), hs] = (acc * _silu(g)).astype(o_ref.dtype)

    def diag_tile(q, q0, hs):
        lb, lom = log2_terms(q, k_ref[pl.ds(q0, tq), hs])
        lom = jnp.where(strict, lom, 0.0)
        tail, rsum = suffix_sums(lom)
        a = jnp.where(strict, jnp.exp2(lb + tail), 0.0)
        return a, rsum

    for hs in heads:
        a, _ = diag_tile(q_ref[pl.ds(0, tq), hs], 0, hs)
        finish(0, hs, _dot(a.astype(BF16), v_ref[pl.ds(0, tq), hs]))

    def more(carry):
        j, rsum, _ = carry
        return jnp.logical_and(j >= 0, jnp.max(rsum) >= SB_STOP_LOG2)

    def q_body(qi, _):
        q0 = pl.multiple_of(qi * tq, tq)
        k1 = pl.multiple_of(q0 - tq, tq)
        qs = [q_ref[pl.ds(q0, tq), hs] for hs in heads]
        terms = [log2_terms(q, k_ref[pl.ds(k1, 2 * tq), hs]) for q, hs in zip(qs, heads)]
        sums = []
        for lb, lom in terms:
            tail_d, rsum_d = suffix_sums(jnp.where(strict, lom[:, tq:], 0.0))
            tail_l, rsum_l = suffix_sums(lom[:, :tq])
            sums.append((tail_l, tail_d, rsum_d, rsum_d + rsum_l))
        accs = []
        for (lb, _), (tail_l, tail_d, rsum_d, _), hs in zip(terms, sums, heads):
            a_l = jnp.exp2(lb[:, :tq] + tail_l + rsum_d)
            a_d = jnp.where(strict, jnp.exp2(lb[:, tq:] + tail_d), 0.0)
            a = jnp.concatenate([a_l, a_d], axis=1).astype(BF16)
            accs.append(_dot(a, v_ref[pl.ds(k1, 2 * tq), hs]))
        gates = [_silu(g_ref[pl.ds(q0, tq), hs].astype(F32)) for hs in heads]
        for hs, acc, gate in zip(heads, accs, gates):
            o_ref[pl.ds(q0, tq), hs] = (acc * gate).astype(o_ref.dtype)

        worst = functools.reduce(jnp.maximum, [s[3] for s in sums])

        @pl.when(jnp.logical_and(qi >= 2, jnp.max(worst) >= SB_STOP_LOG2))
        def _visit_more_tiles():
            for hs, q, (_, _, _, rsum), acc, gate in zip(heads, qs, sums, accs, gates):
                def tile(carry, q=q, hs=hs):
                    j, rsum, acc = carry
                    k0 = pl.multiple_of(j * tq, tq)
                    lb, lom = log2_terms(q, k_ref[pl.ds(k0, tq), hs])
                    tail, rs = suffix_sums(lom)
                    a = jnp.exp2(lb + tail + rsum)
                    acc = acc + _dot(a.astype(BF16), v_ref[pl.ds(k0, tq), hs])
                    return j - 1, rsum + rs, acc

                acc = lax.while_loop(more, tile, (qi - 2, rsum, acc))[2]
                o_ref[pl.ds(q0, tq), hs] = (acc * gate).astype(o_ref.dtype)
        return 0

    lax.fori_loop(1, s_len // tq, q_body, 0)


def _sb_attention(proj, bsz, s_len, n_heads, col_blocks, tq=256, hp=4):
    qb, kb, vb, gb = col_blocks
    scale2 = LOG2E * HEAD_DIM ** -0.5
    assert n_heads % hp == 0 and all(off % hp == 0 for off in col_blocks)

    def head_spec(off):
        return pl.BlockSpec((s_len, hp * HEAD_DIM), lambda b, h: (b, off // hp + h))

    return pl.pallas_call(
        functools.partial(_sb_kernel, tq=tq, scale2=scale2, hp=hp),
        grid=(bsz, n_heads // hp),
        in_specs=[head_spec(qb), head_spec(kb), head_spec(vb), head_spec(gb)],
        out_specs=pl.BlockSpec((s_len, hp * HEAD_DIM), lambda b, h: (b, h)),
        out_shape=jax.ShapeDtypeStruct((bsz * s_len, n_heads * HEAD_DIM), BF16),
        compiler_params=_params(2),
        name="sb_attention",
    )(proj, proj, proj, proj)


def _even_layer(x2, h, bsz, s_len, g_post, g_next, w_in, b_f, rel_bias, w_out):
    n_a = b_f.shape[0]
    n_b = rel_bias.shape[0]
    main_cols = 4 * (n_a + n_b) * HEAD_DIM
    w_in_t = w_in.T
    proj = _matmul([h], w_in_t, BF16, bm=PROJ_BM, bn=PROJ_BN, n_col_blocks=main_cols // PROJ_BN,
                   w_transposed=True, name="in_proj_even")
    af, aft = _forget_proj(h, w_in_t, main_cols, n_a)
    c_col, c_row = _forget_cumsum(af, aft, b_f, bsz)
    mix_a = _fox_attention(proj, c_row, c_col, bsz, s_len, n_a,
                           (0, n_a, 2 * n_a, 3 * n_a))
    off = 4 * n_a
    mix_b = _chunk_attention(proj, rel_bias, bsz, s_len, n_b,
                             (off, off + n_b, off + 2 * n_b, off + 3 * n_b))
    return _out_proj_norm([mix_a, mix_b], w_out.astype(BF16), x2, g_post, g_next,
                          bm=OUT_BM, bn=OUT_BN, name="out_proj_norm_even")


def _odd_layer(x2, h, bsz, s_len, g_post, g_next, w_in, w_out):
    n_c = w_out.shape[0] // HEAD_DIM
    proj = _matmul([h], w_in, BF16, bm=PROJ_BM, bn=PROJ_BN, name="in_proj_odd")
    mix = _sb_attention(proj, bsz, s_len, n_c, (0, n_c, 2 * n_c, 3 * n_c))
    return _out_proj_norm([mix], w_out.astype(BF16), x2, g_post, g_next,
                          bm=OUT_BM, bn=OUT_BN, name="out_proj_norm_odd")


def kernel(x, norm_pre, norm_post, w_in_even, b_f_even, rel_bias_even, w_out_even, w_in_odd, w_out_odd):
    bsz, s_len, d = x.shape
    depth = norm_pre.shape[0]
    x2 = x.reshape(bsz * s_len, d)
    h = _rmsnorm(x2, norm_pre[0], BF16)
    for layer in range(depth):
        i = layer // 2
        g_next = norm_pre[layer + 1] if layer + 1 < depth else None
        if layer % 2 == 0:
            x2, h = _even_layer(x2, h, bsz, s_len, norm_post[layer], g_next,
                                w_in_even[i], b_f_even[i], rel_bias_even[i], w_out_even[i])
        else:
            x2, h = _odd_layer(x2, h, bsz, s_len, norm_post[layer], g_next,
                               w_in_odd[i], w_out_odd[i])
    return x2.reshape(bsz, s_len, d)
```

```python
import functools

import jax
import jax.numpy as jnp
from jax import lax
from jax.experimental import pallas as pl
from jax.experimental.pallas import tpu as pltpu

HEAD_DIM = 128
CHUNK = 64
LEFT_CHUNKS = 8
REL_CLIP = 128
N_REL = 2 * REL_CLIP + 1
RMS_EPS = 1e-6

LANES = 128
VMEM_LIMIT = 60 * 1024 * 1024
PROJ_BM, PROJ_BN = 512, 1024
OUT_BM, OUT_BN = 512, 512
NORM_ROWS = 128
NEG = -1e30
LOG2E = 1.4426950408889634
SB_STOP_LOG2 = -152.0

BF16 = jnp.bfloat16
F32 = jnp.float32


def _params(n_axes):
    return pltpu.CompilerParams(dimension_semantics=("arbitrary",) * n_axes,
                                vmem_limit_bytes=VMEM_LIMIT)


def _dot_nt(a, b):
    return lax.dot_general(a, b, (((1,), (1,)), ((), ())), preferred_element_type=F32)


def _dot(a, b):
    return jnp.dot(a, b, preferred_element_type=F32)


def _split3(x):
    hi = x.astype(BF16)
    r1 = x - hi.astype(F32)
    mid = r1.astype(BF16)
    lo = (r1 - mid.astype(F32)).astype(BF16)
    return hi, mid, lo


def _log_sigmoid(z):
    return jnp.minimum(z, 0.0) - jnp.log1p(jnp.exp(-jnp.abs(z)))


def _silu(g):
    return g / (1.0 + jnp.exp(-g))


def _rmsnorm_kernel(x_ref, g_ref, o_ref):
    x = x_ref[...]
    ms = jnp.mean(x * x, axis=-1, keepdims=True)
    o_ref[...] = (x * lax.rsqrt(ms + RMS_EPS) * g_ref[...]).astype(o_ref.dtype)


def _rmsnorm(x, g, out_dtype, bm=256):
    m, d = x.shape
    return pl.pallas_call(
        _rmsnorm_kernel,
        grid=(m // bm,),
        in_specs=[pl.BlockSpec((bm, d), lambda i: (i, 0)),
                  pl.BlockSpec((1, d), lambda i: (0, 0))],
        out_specs=pl.BlockSpec((bm, d), lambda i: (i, 0)),
        out_shape=jax.ShapeDtypeStruct((m, d), out_dtype),
        compiler_params=_params(1),
        name="rmsnorm",
    )(x, g.reshape(1, d))


def _out_proj_norm_kernel(*refs, n_parts, bn, with_next):
    mix_refs, w_refs = refs[:n_parts], refs[n_parts:2 * n_parts]
    rest = list(refs[2 * n_parts:])
    xres_ref, gpost_ref = rest.pop(0), rest.pop(0)
    gnext_ref = rest.pop(0) if with_next else None
    o_ref = rest.pop(0)
    h_ref = rest.pop(0) if with_next else None
    xs_ref = rest.pop(0)
    j = pl.program_id(1)
    c0 = pl.multiple_of(j * bn, bn)

    y = _dot(mix_refs[0][...], w_refs[0][...])
    for mix_ref, w_ref in zip(mix_refs[1:], w_refs[1:]):
        y = y + _dot(mix_ref[...], w_ref[...])
    o_ref[:, pl.ds(c0, bn)] = y
    xs_ref[:, pl.ds(c0, bn)] = xres_ref[...]

    @pl.when(j == pl.num_programs(1) - 1)
    def _normalise():
        def rows(r, _):
            rs = pl.ds(pl.multiple_of(r * NORM_ROWS, NORM_ROWS), NORM_ROWS)
            y = o_ref[rs, :]
            ms = jnp.mean(y * y, axis=-1, keepdims=True)
            x_new = xs_ref[rs, :] + y * lax.rsqrt(ms + RMS_EPS) * gpost_ref[...]
            o_ref[rs, :] = x_new
            if with_next:
                ms = jnp.mean(x_new * x_new, axis=-1, keepdims=True)
                h_ref[rs, :] = (x_new * lax.rsqrt(ms + RMS_EPS) * gnext_ref[...]).astype(h_ref.dtype)
            return 0

        lax.fori_loop(0, o_ref.shape[0] // NORM_ROWS, rows, 0)


def _out_proj_norm(mixes, w16, x, g_post, g_next, *, bm, bn, name):
    m, d = x.shape
    kp = mixes[0].shape[1]
    n_parts = len(mixes)
    with_next = g_next is not None
    assert all(t.shape == (m, kp) for t in mixes) and w16.shape == (n_parts * kp, d)
    row_block = pl.BlockSpec((bm, d), lambda i, j: (i, 0))
    gain = pl.BlockSpec((1, d), lambda i, j: (0, 0))
    in_specs = ([pl.BlockSpec((bm, kp), lambda i, j: (i, 0)) for _ in mixes]
                + [pl.BlockSpec((kp, bn), lambda i, j, p=p: (p, j)) for p in range(n_parts)]
                + [pl.BlockSpec((bm, bn), lambda i, j: (i, j)), gain] + ([gain] if with_next else []))
    operands = list(mixes) + [w16] * n_parts + [x, g_post.reshape(1, d)]
    out_specs, out_shape = [row_block], [jax.ShapeDtypeStruct((m, d), F32)]
    if with_next:
        operands.append(g_next.reshape(1, d))
        out_specs.append(row_block)
        out_shape.append(jax.ShapeDtypeStruct((m, d), BF16))
    outs = pl.pallas_call(
        functools.partial(_out_proj_norm_kernel, n_parts=n_parts, bn=bn, with_next=with_next),
        grid=(m // bm, d // bn),
        in_specs=in_specs,
        out_specs=out_specs,
        out_shape=out_shape,
        scratch_shapes=[pltpu.VMEM((bm, d), F32)],
        compiler_params=_params(2),
        name=name,
    )(*operands)
    return (outs[0], outs[1]) if with_next else (outs[0], None)


def _matmul_kernel(*refs, n_parts, w_transposed):
    x_refs, w_refs = refs[:n_parts], refs[n_parts:2 * n_parts]
    o_ref = refs[2 * n_parts]
    w16_refs = refs[2 * n_parts + 1:]
    dot = _dot_nt if w_transposed else _dot

    @pl.when(pl.program_id(1) == 0)
    def _round_weights():
        for w_ref, w16_ref in zip(w_refs, w16_refs):
            w16_ref[...] = w_ref[...].astype(BF16)

    acc = dot(x_refs[0][...], w16_refs[0][...])
    for x_ref, w16_ref in zip(x_refs[1:], w16_refs[1:]):
        acc = acc + dot(x_ref[...], w16_ref[...])
    o_ref[...] = acc.astype(o_ref.dtype)


def _matmul(xs, w, out_dtype, *, bm, bn, n_col_blocks=None, w_transposed=False, name="matmul"):
    m = xs[0].shape[0]
    kp = xs[0].shape[1]
    n_parts = len(xs)
    k_axis, n_axis = (1, 0) if w_transposed else (0, 1)
    assert all(x.shape == (m, kp) for x in xs) and w.shape[k_axis] == n_parts * kp
    if n_col_blocks is None:
        n_col_blocks = w.shape[n_axis] // bn
    w_block = (bn, kp) if w_transposed else (kp, bn)

    def w_spec(p):
        return pl.BlockSpec(w_block, (lambda j, i: (j, p)) if w_transposed else (lambda j, i: (p, j)))

    return pl.pallas_call(
        functools.partial(_matmul_kernel, n_parts=n_parts, w_transposed=w_transposed),
        grid=(n_col_blocks, m // bm),
        in_specs=[pl.BlockSpec((bm, kp), lambda j, i: (i, 0)) for _ in xs]
                 + [w_spec(p) for p in range(n_parts)],
        out_specs=pl.BlockSpec((bm, bn), lambda j, i: (i, j)),
        out_shape=jax.ShapeDtypeStruct((m, n_col_blocks * bn), out_dtype),
        scratch_shapes=[pltpu.VMEM(w_block, BF16) for _ in xs],
        compiler_params=_params(2),
        name=name,
    )(*xs, *([w] * n_parts))


def _forget_proj_kernel(x_ref, wt_ref, af_ref, aft_ref):
    w16 = wt_ref[...].astype(BF16)
    x = x_ref[...]
    af_ref[...] = _dot_nt(x, w16)
    aft_ref[...] = _dot_nt(w16, x)


def _forget_proj(x, w_t, row0, n_rows, bm=1024):
    m, k = x.shape
    assert row0 % n_rows == 0 and w_t.shape[1] == k
    return pl.pallas_call(
        _forget_proj_kernel,
        grid=(m // bm,),
        in_specs=[pl.BlockSpec((bm, k), lambda i: (i, 0)),
                  pl.BlockSpec((n_rows, k), lambda i: (row0 // n_rows, 0))],
        out_specs=[pl.BlockSpec((bm, n_rows), lambda i: (i, 0)),
                   pl.BlockSpec((n_rows, bm), lambda i: (0, i))],
        out_shape=[jax.ShapeDtypeStruct((m, n_rows), F32),
                   jax.ShapeDtypeStruct((n_rows, m), F32)],
        compiler_params=_params(1),
        name="forget_proj",
    )(x, w_t)


def _forget_cumsum_kernel(af_ref, aft_ref, bcol_ref, brow_ref, ccol_ref, crow_ref, *, tc):
    s_len, nh = af_ref.shape
    r = lax.broadcasted_iota(jnp.int32, (tc, tc), 0)
    c = lax.broadcasted_iota(jnp.int32, (tc, tc), 1)
    lower = (c <= r).astype(F32).astype(BF16)
    upper = (r <= c).astype(F32).astype(BF16)

    def body(i, carry):
        carry_row, carry_col = carry
        t0 = pl.multiple_of(i * tc, tc)
        lf = _log_sigmoid(af_ref[pl.ds(t0, tc), :] + brow_ref[...])
        lft = _log_sigmoid(aft_ref[:, pl.ds(t0, tc)] + bcol_ref[...])
        a, b, d = _split3(lf)
        cc = (_dot(lower, a) + _dot(lower, b) + _dot(lower, d)) + carry_row
        a, b, d = _split3(lft)
        cr = (_dot(a, upper) + _dot(b, upper) + _dot(d, upper)) + carry_col
        ccol_ref[pl.ds(t0, tc), :] = cc * LOG2E
        crow_ref[:, pl.ds(t0, tc)] = cr * LOG2E
        return cc[tc - 1:tc, :], cr[:, tc - 1:tc]

    lax.fori_loop(0, s_len // tc, body,
                  (jnp.zeros((1, nh), F32), jnp.zeros((nh, 1), F32)))


def _forget_cumsum(af, aft, b_f, bsz, tc=256):
    m, nh = af.shape
    s_len = m // bsz
    return pl.pallas_call(
        functools.partial(_forget_cumsum_kernel, tc=tc),
        grid=(bsz,),
        in_specs=[pl.BlockSpec((s_len, nh), lambda b: (b, 0)),
                  pl.BlockSpec((nh, s_len), lambda b: (0, b)),
                  pl.BlockSpec((nh, 1), lambda b: (0, 0)),
                  pl.BlockSpec((1, nh), lambda b: (0, 0))],
        out_specs=[pl.BlockSpec((s_len, nh), lambda b: (b, 0)),
                   pl.BlockSpec((nh, s_len), lambda b: (0, b))],
        out_shape=[jax.ShapeDtypeStruct((m, nh), F32),
                   jax.ShapeDtypeStruct((nh, m), F32)],
        compiler_params=_params(1),
        name="forget_cumsum",
    )(af, aft, b_f.reshape(nh, 1), b_f.reshape(1, nh))


def _fox_kernel(q_ref, k_ref, v_ref, g_ref, crow_ref, ccol_ref, o_ref, vone_ref, *, tq, tk, scale2, hp):
    h0 = pl.program_id(1) * hp
    s_len = q_ref.shape[0]
    lane = lax.broadcasted_iota(jnp.int32, (tq, ccol_ref.shape[-1]), 1)
    heads = [slice(i * HEAD_DIM, (i + 1) * HEAD_DIM) for i in range(hp)]
    wide = [slice(2 * i * HEAD_DIM, 2 * (i + 1) * HEAD_DIM) for i in range(hp)]
    for hs, ws in zip(heads, wide):
        vone_ref[:, ws] = jnp.concatenate(
            [v_ref[:, hs], jnp.ones((s_len, HEAD_DIM), vone_ref.dtype)], axis=1)

    def q_body(qi, _):
        q0 = pl.multiple_of(qi * tq, tq)
        qs = [q_ref[pl.ds(q0, tq), hs] for hs in heads]
        c_blk = ccol_ref[pl.ds(q0, tq), :]
        c_ts = [jnp.sum(jnp.where(lane == h0 + i, c_blk, 0.0), axis=1, keepdims=True) for i in range(hp)]
        n_full = q0 // tk

        def logits(k0, width):
            return tuple(_dot_nt(qs[i], k_ref[pl.ds(k0, width), hs]) * scale2 - crow_ref[i, :, pl.ds(k0, width)]
                         for i, hs in enumerate(heads))

        def absorb(us, k0, width, state, diag_offset=None):
            if diag_offset is not None:
                keep = (lax.broadcasted_iota(jnp.int32, (tq, width), 1)
                        <= lax.broadcasted_iota(jnp.int32, (tq, width), 0) + diag_offset)
                us = [jnp.where(keep, u, NEG) for u in us]
            m_news = [jnp.maximum(st[0], jnp.max(u, axis=1, keepdims=True) + c_t)
                      for u, st, c_t in zip(us, state, c_ts)]
            ps = [jnp.exp2(u - (m_new - c_t)).astype(BF16) for u, m_new, c_t in zip(us, m_news, c_ts)]
            pvs = [_dot(p, vone_ref[pl.ds(k0, width), ws]) for p, ws in zip(ps, wide)]
            return tuple((m_new, jnp.exp2(m - m_new) * acc + pv)
                         for (m, acc), m_new, pv in zip(state, m_news, pvs))

        def absorb_tiles(tiles, state):
            all_us = [logits(k0, width) for k0, width, _ in tiles]
            for us, (k0, width, diag_offset) in zip(all_us, tiles):
                state = absorb(us, k0, width, state, diag_offset)
            return state

        def step2(kp, state):
            k0 = pl.multiple_of(2 * kp * tk, tk)
            return absorb_tiles([(k0, tk, None), (k0 + tk, tk, None)], state)

        state = tuple((jnp.full((tq, 1), NEG, F32), jnp.zeros((tq, 2 * HEAD_DIM), F32)) for _ in heads)
        state = lax.fori_loop(0, n_full // 2, step2, state)
        k_diag = pl.multiple_of(n_full * tk, tk)
        n_r = tk // tq

        def last_branch(odd, r):
            width = (r + 1) * tq

            def run():
                gates = [_silu(g_ref[pl.ds(q0, tq), hs].astype(F32)) for hs in heads]
                tiles = [(pl.multiple_of(k_diag - tk, tk), tk, None)] if odd else []
                final = absorb_tiles(tiles + [(k_diag, width, r * tq)], state)
                for (_, acc), gate, hs in zip(final, gates, heads):
                    o_ref[pl.ds(q0, tq), hs] = ((acc[:, :HEAD_DIM] / acc[:, HEAD_DIM:]) * gate).astype(o_ref.dtype)
                return 0

            return run

        return lax.switch((n_full % 2) * n_r + qi % n_r,
                          [last_branch(odd, r) for odd in (0, 1) for r in range(n_r)])

    lax.fori_loop(0, s_len // tq, q_body, 0)


def _fox_attention(proj, c_row, c_col, bsz, s_len, n_heads, col_blocks, tq=256, tk=1024, hp=2):
    qb, kb, vb, gb = col_blocks
    scale2 = LOG2E * HEAD_DIM ** -0.5
    assert n_heads % hp == 0 and all(off % hp == 0 for off in col_blocks)
    assert tk % tq == 0 and s_len % tk == 0

    def head_spec(off):
        return pl.BlockSpec((s_len, hp * HEAD_DIM), lambda b, h: (b, off // hp + h))

    return pl.pallas_call(
        functools.partial(_fox_kernel, tq=tq, tk=tk, scale2=scale2, hp=hp),
        grid=(bsz, n_heads // hp),
        in_specs=[head_spec(qb), head_spec(kb), head_spec(vb), head_spec(gb),
                  pl.BlockSpec((hp, 1, s_len), lambda b, h: (h, 0, b)),
                  pl.BlockSpec((s_len, n_heads), lambda b, h: (b, 0))],
        out_specs=pl.BlockSpec((s_len, hp * HEAD_DIM), lambda b, h: (b, h)),
        out_shape=jax.ShapeDtypeStruct((bsz * s_len, n_heads * HEAD_DIM), BF16),
        scratch_shapes=[pltpu.VMEM((s_len, hp * 2 * HEAD_DIM), BF16)],
        compiler_params=_params(2),
        name="fox_attention",
    )(proj, proj, proj, proj, c_row.reshape(n_heads, 1, bsz * s_len), c_col)


def _chunk_kernel(q_ref, k_ref, v_ref, g_ref, rel_ref, o_ref, kpad, vpad, bias_ref, *, tq, scale, hp):
    b = pl.program_id(1)
    s_len = q_ref.shape[0]
    pad = LEFT_CHUNKS * CHUNK
    win = tq + pad
    rp = rel_ref.shape[-1]
    heads = [slice(i * HEAD_DIM, (i + 1) * HEAD_DIM) for i in range(hp)]

    @pl.when(b == 0)
    def _build_bias():
        u = lax.broadcasted_iota(jnp.int32, (rp, win), 1)
        r = lax.broadcasted_iota(jnp.int32, (rp, win), 0)
        idx = jnp.where(u < pad + CHUNK, jnp.clip(pad - u, -REL_CLIP, REL_CLIP) + REL_CLIP, 2 * REL_CLIP)
        sel = (r == idx).astype(F32).astype(BF16)
        qc = lax.broadcasted_iota(jnp.int32, (tq, win), 0) // CHUNK
        kc = lax.broadcasted_iota(jnp.int32, (tq, win), 1) // CHUNK
        in_band = (kc >= qc) & (kc <= qc + LEFT_CHUNKS)
        for i in range(hp):
            a, bb, d = _split3(jnp.broadcast_to(rel_ref[i], (8, rp)))
            g = _dot(a, sel) + _dot(bb, sel) + _dot(d, sel)
            tile = pltpu.roll(jnp.broadcast_to(g[0:1, :], (tq, win)), 0, 1, stride=1, stride_axis=0)
            bias_ref[i] = jnp.where(in_band, tile, NEG)

    kpad[pl.ds(0, pad), :] = jnp.zeros((pad, hp * HEAD_DIM), kpad.dtype)
    kpad[pl.ds(pad, s_len), :] = k_ref[...]
    wide = [slice(2 * i * HEAD_DIM, 2 * (i + 1) * HEAD_DIM) for i in range(hp)]
    vpad[pl.ds(0, pad), :] = jnp.zeros((pad, 2 * hp * HEAD_DIM), vpad.dtype)
    for hs, ws in zip(heads, wide):
        vpad[pl.ds(pad, s_len), ws] = jnp.concatenate(
            [v_ref[:, hs], jnp.ones((s_len, HEAD_DIM), vpad.dtype)], axis=1)
    key_pos = lax.broadcasted_iota(jnp.int32, (tq, win), 1)

    def body(j, _, before_start):
        q0 = pl.multiple_of(j * tq, tq)
        ss = [_dot_nt(q_ref[pl.ds(q0, tq), hs], kpad[pl.ds(q0, win), hs]) * scale + bias_ref[i]
              for i, hs in enumerate(heads)]
        if before_start:
            ss = [jnp.where(key_pos + q0 >= pad, s, NEG) for s in ss]
        ps = [jnp.exp(s - jnp.max(s, axis=1, keepdims=True)).astype(BF16) for s in ss]
        pvs = [_dot(p, vpad[pl.ds(q0, win), ws]) for p, ws in zip(ps, wide)]
        for pv, hs in zip(pvs, heads):
            o = pv[:, :HEAD_DIM] / pv[:, HEAD_DIM:]
            g = g_ref[pl.ds(q0, tq), hs].astype(F32)
            o_ref[pl.ds(q0, tq), hs] = (o * _silu(g)).astype(o_ref.dtype)
        return 0

    n_start = pad // tq
    lax.fori_loop(0, n_start, functools.partial(body, before_start=True), 0)
    lax.fori_loop(n_start, s_len // tq, functools.partial(body, before_start=False), 0)


def _chunk_attention(proj, rel_bias, bsz, s_len, n_heads, col_blocks, tq=256, hp=4):
    qb, kb, vb, gb = col_blocks
    scale = HEAD_DIM ** -0.5
    pad = LEFT_CHUNKS * CHUNK
    rp = 3 * LANES
    assert N_REL <= rp and pad % tq == 0
    assert n_heads % hp == 0 and all(off % hp == 0 for off in col_blocks)
    rel = jnp.pad(rel_bias, ((0, 0), (0, rp - N_REL))).reshape(n_heads, 1, rp)

    def head_spec(off):
        return pl.BlockSpec((s_len, hp * HEAD_DIM), lambda h, b: (b, off // hp + h))

    return pl.pallas_call(
        functools.partial(_chunk_kernel, tq=tq, scale=scale, hp=hp),
        grid=(n_heads // hp, bsz),
        in_specs=[head_spec(qb), head_spec(kb), head_spec(vb), head_spec(gb),
                  pl.BlockSpec((hp, 1, rp), lambda h, b: (h, 0, 0))],
        out_specs=pl.BlockSpec((s_len, hp * HEAD_DIM), lambda h, b: (b, h)),
        out_shape=jax.ShapeDtypeStruct((bsz * s_len, n_heads * HEAD_DIM), BF16),
        scratch_shapes=[pltpu.VMEM((s_len + pad, hp * HEAD_DIM), BF16),
                        pltpu.VMEM((s_len + pad, 2 * hp * HEAD_DIM), BF16),
                        pltpu.VMEM((hp, tq, tq + pad), F32)],
        compiler_params=_params(2),
        name="chunk_attention",
    )(proj, proj, proj, proj, rel)


def _sb_kernel(q_ref, k_ref, v_ref, g_ref, o_ref, *, ts, tl, scale2, hp):
    s_len = q_ref.shape[0]
    heads = [slice(i * HEAD_DIM, (i + 1) * HEAD_DIM) for i in range(hp)]

    def after2(n):
        a = lax.broadcasted_iota(jnp.int32, (n, n), 0) > lax.broadcasted_iota(jnp.int32, (n, n), 1)
        a = a.astype(F32).astype(BF16)
        return jnp.concatenate([a, a], axis=0)

    tri = {n: after2(n) for n in {ts, tl}}
    strict = lax.broadcasted_iota(jnp.int32, (ts, ts), 1) < lax.broadcasted_iota(jnp.int32, (ts, ts), 0)

    def log2_terms(q, k):
        z = _dot_nt(q, k) * scale2
        lb = jnp.minimum(z, 0.0) - jnp.log2(1.0 + jnp.exp2(-jnp.abs(z)))
        return lb, lb - z

    def suffix_sums(lom):
        hi = lom.astype(BF16)
        lo = (lom - hi.astype(F32)).astype(BF16)
        tail = _dot(jnp.concatenate([hi, lo], axis=1), tri[lom.shape[1]])
        return tail, tail[:, 0:1] + lom[:, 0:1]

    def first_visit(chains):
        rows = [pl.ds(r0, ts) for _, r0, _ in chains]
        keys = [pl.ds(r0 - lw, lw + ts) for _, r0, lw in chains]
        qs = [q_ref[rs, hs] for rs, (hs, _, _) in zip(rows, chains)]
        terms = [log2_terms(q, k_ref[ks, hs]) for q, ks, (hs, _, _) in zip(qs, keys, chains)]
        sums = []
        for (lb, lom), (_, _, lw) in zip(terms, chains):
            tail_d, rsum = suffix_sums(jnp.where(strict, lom[:, lw:], 0.0))
            tail_l = None
            if lw:
                tail_l, rsum_l = suffix_sums(lom[:, :lw])
                tail_l, rsum = tail_l + rsum, rsum + rsum_l
            sums.append((tail_l, tail_d, rsum))
        accs = []
        for (lb, _), (tail_l, tail_d, _), ks, (hs, _, lw) in zip(terms, sums, keys, chains):
            a = jnp.where(strict, jnp.exp2(lb[:, lw:] + tail_d), 0.0)
            if lw:
                a = jnp.concatenate([jnp.exp2(lb[:, :lw] + tail_l), a], axis=1)
            accs.append(_dot(a.astype(BF16), v_ref[ks, hs]))
        gates = [_silu(g_ref[rs, hs].astype(F32)) for rs, (hs, _, _) in zip(rows, chains)]
        for rs, (hs, _, _), acc, gate in zip(rows, chains, accs, gates):
            o_ref[rs, hs] = (acc * gate).astype(o_ref.dtype)
        return [(q, s[2], acc, gate) for q, s, acc, gate in zip(qs, sums, accs, gates)]

    for lw in range(0, tl, ts):
        first_visit([(hs, lw, lw) for hs in heads])

    def more(carry):
        k0, rsum, _ = carry
        return jnp.logical_and(k0 >= 0, jnp.max(rsum) >= SB_STOP_LOG2)

    def q_body(it, _):
        q0 = pl.multiple_of(tl + it * tl, tl)
        chains = [(hs, q0 + off, tl) for off in range(0, tl, ts) for hs in heads]
        carries = first_visit(chains)

        worst = functools.reduce(jnp.maximum, [c[1] for c in carries])

        @pl.when(jnp.max(worst) >= SB_STOP_LOG2)
        def _visit_more_keys():
            for (hs, r0, lw), (q, rsum, acc, gate) in zip(chains, carries):
                def tile(carry, q=q, hs=hs):
                    k0, rsum, acc = carry
                    ks = pl.ds(pl.multiple_of(k0, ts), ts)
                    lb, lom = log2_terms(q, k_ref[ks, hs])
                    tail, rs = suffix_sums(lom)
                    a = jnp.exp2(lb + tail + rsum)
                    return k0 - ts, rsum + rs, acc + _dot(a.astype(BF16), v_ref[ks, hs])

                acc = lax.while_loop(more, tile, (r0 - lw - ts, rsum, acc))[2]
                o_ref[pl.ds(r0, ts), hs] = (acc * gate).astype(o_ref.dtype)
        return 0

    lax.fori_loop(0, s_len // tl - 1, q_body, 0)


def _sb_attention(proj, bsz, s_len, n_heads, col_blocks, ts=128, tl=256, hp=4):
    qb, kb, vb, gb = col_blocks
    scale2 = LOG2E * HEAD_DIM ** -0.5
    assert n_heads % hp == 0 and all(off % hp == 0 for off in col_blocks)
    assert tl % ts == 0 and s_len % tl == 0

    def head_spec(off):
        return pl.BlockSpec((s_len, hp * HEAD_DIM), lambda b, h: (b, off // hp + h))

    return pl.pallas_call(
        functools.partial(_sb_kernel, ts=ts, tl=tl, scale2=scale2, hp=hp),
        grid=(bsz, n_heads // hp),
        in_specs=[head_spec(qb), head_spec(kb), head_spec(vb), head_spec(gb)],
        out_specs=pl.BlockSpec((s_len, hp * HEAD_DIM), lambda b, h: (b, h)),
        out_shape=jax.ShapeDtypeStruct((bsz * s_len, n_heads * HEAD_DIM), BF16),
        compiler_params=_params(2),
        name="sb_attention",
    )(proj, proj, proj, proj)


def _even_layer(x2, h, bsz, s_len, g_post, g_next, w_in, b_f, rel_bias, w_out):
    n_a = b_f.shape[0]
    n_b = rel_bias.shape[0]
    main_cols = 4 * (n_a + n_b) * HEAD_DIM
    w_in_t = w_in.T
    proj = _matmul([h], w_in_t, BF16, bm=PROJ_BM, bn=PROJ_BN, n_col_blocks=main_cols // PROJ_BN,
                   w_transposed=True, name="in_proj_even")
    af, aft = _forget_proj(h, w_in_t, main_cols, n_a)
    c_col, c_row = _forget_cumsum(af, aft, b_f, bsz)
    mix_a = _fox_attention(proj, c_row, c_col, bsz, s_len, n_a,
                           (0, n_a, 2 * n_a, 3 * n_a))
    off = 4 * n_a
    mix_b = _chunk_attention(proj, rel_bias, bsz, s_len, n_b,
                             (off, off + n_b, off + 2 * n_b, off + 3 * n_b))
    return _out_proj_norm([mix_a, mix_b], w_out.astype(BF16), x2, g_post, g_next,
                          bm=OUT_BM, bn=OUT_BN, name="out_proj_norm_even")


def _odd_layer(x2, h, bsz, s_len, g_post, g_next, w_in, w_out):
    n_c = w_out.shape[0] // HEAD_DIM
    proj = _matmul([h], w_in, BF16, bm=PROJ_BM, bn=PROJ_BN, name="in_proj_odd")
    mix = _sb_attention(proj, bsz, s_len, n_c, (0, n_c, 2 * n_c, 3 * n_c))
    return _out_proj_norm([mix], w_out.astype(BF16), x2, g_post, g_next,
                          bm=OUT_BM, bn=OUT_BN, name="out_proj_norm_odd")


def kernel(x, norm_pre, norm_post, w_in_even, b_f_even, rel_bias_even, w_out_even, w_in_odd, w_out_odd):
    bsz, s_len, d = x.shape
    depth = norm_pre.shape[0]
    x2 = x.reshape(bsz * s_len, d)
    h = _rmsnorm(x2, norm_pre[0], BF16)
    for layer in range(depth):
        i = layer // 2
        g_next = norm_pre[layer + 1] if layer + 1 < depth else None
        if layer % 2 == 0:
            x2, h = _even_layer(x2, h, bsz, s_len, norm_post[layer], g_next,
                                w_in_even[i], b_f_even[i], rel_bias_even[i], w_out_even[i])
        else:
            x2, h = _odd_layer(x2, h, bsz, s_len, norm_post[layer], g_next,
                               w_in_odd[i], w_out_odd[i])
    return x2.reshape(bsz, s_len, d)
```

```python
import functools

import jax
import jax.numpy as jnp
from jax import lax
from jax.experimental import pallas as pl
from jax.experimental.pallas import tpu as pltpu

HEAD_DIM = 128
CHUNK = 64
LEFT_CHUNKS = 8
REL_CLIP = 128
N_REL = 2 * REL_CLIP + 1
RMS_EPS = 1e-6

LANES = 128
VMEM_LIMIT = 60 * 1024 * 1024
PROJ_BM, PROJ_BN = 512, 1024
OUT_BM, OUT_BN = 512, 512
NORM_ROWS = 128
NEG = -1e30
LOG2E = 1.4426950408889634
SB_STOP_LOG2 = -152.0

BF16 = jnp.bfloat16
F32 = jnp.float32


def _params(n_axes):
    return pltpu.CompilerParams(dimension_semantics=("arbitrary",) * n_axes,
                                vmem_limit_bytes=VMEM_LIMIT)


def _dot_nt(a, b):
    return lax.dot_general(a, b, (((1,), (1,)), ((), ())), preferred_element_type=F32)


def _dot(a, b):
    return jnp.dot(a, b, preferred_element_type=F32)


def _split3(x):
    hi = x.astype(BF16)
    r1 = x - hi.astype(F32)
    mid = r1.astype(BF16)
    lo = (r1 - mid.astype(F32)).astype(BF16)
    return hi, mid, lo


def _log_sigmoid(z):
    return jnp.minimum(z, 0.0) - jnp.log1p(jnp.exp(-jnp.abs(z)))


def _silu(g):
    return g / (1.0 + jnp.exp(-g))


def _rmsnorm_kernel(x_ref, g_ref, o_ref):
    x = x_ref[...]
    ms = jnp.mean(x * x, axis=-1, keepdims=True)
    o_ref[...] = (x * lax.rsqrt(ms + RMS_EPS) * g_ref[...]).astype(o_ref.dtype)


def _rmsnorm(x, g, out_dtype, bm=256):
    m, d = x.shape
    return pl.pallas_call(
        _rmsnorm_kernel,
        grid=(m // bm,),
        in_specs=[pl.BlockSpec((bm, d), lambda i: (i, 0)),
                  pl.BlockSpec((1, d), lambda i: (0, 0))],
        out_specs=pl.BlockSpec((bm, d), lambda i: (i, 0)),
        out_shape=jax.ShapeDtypeStruct((m, d), out_dtype),
        compiler_params=_params(1),
        name="rmsnorm",
    )(x, g.reshape(1, d))


def _out_proj_norm_kernel(*refs, n_parts, bn, with_next):
    mix_refs, w_refs = refs[:n_parts], refs[n_parts:2 * n_parts]
    rest = list(refs[2 * n_parts:])
    xres_ref, gpost_ref = rest.pop(0), rest.pop(0)
    gnext_ref = rest.pop(0) if with_next else None
    o_ref = rest.pop(0)
    h_ref = rest.pop(0) if with_next else None
    xs_ref = rest.pop(0)
    j = pl.program_id(1)
    c0 = pl.multiple_of(j * bn, bn)

    y = _dot(mix_refs[0][...], w_refs[0][...])
    for mix_ref, w_ref in zip(mix_refs[1:], w_refs[1:]):
        y = y + _dot(mix_ref[...], w_ref[...])
    o_ref[:, pl.ds(c0, bn)] = y
    xs_ref[:, pl.ds(c0, bn)] = xres_ref[...]

    @pl.when(j == pl.num_programs(1) - 1)
    def _normalise():
        def rows(r, _):
            rs = pl.ds(pl.multiple_of(r * NORM_ROWS, NORM_ROWS), NORM_ROWS)
            y = o_ref[rs, :]
            ms = jnp.mean(y * y, axis=-1, keepdims=True)
            x_new = xs_ref[rs, :] + y * lax.rsqrt(ms + RMS_EPS) * gpost_ref[...]
            o_ref[rs, :] = x_new
            if with_next:
                ms = jnp.mean(x_new * x_new, axis=-1, keepdims=True)
                h_ref[rs, :] = (x_new * lax.rsqrt(ms + RMS_EPS) * gnext_ref[...]).astype(h_ref.dtype)
            return 0

        lax.fori_loop(0, o_ref.shape[0] // NORM_ROWS, rows, 0)


def _out_proj_norm(mixes, w16, x, g_post, g_next, *, bm, bn, name):
    m, d = x.shape
    kp = mixes[0].shape[1]
    n_parts = len(mixes)
    with_next = g_next is not None
    assert all(t.shape == (m, kp) for t in mixes) and w16.shape == (n_parts * kp, d)
    row_block = pl.BlockSpec((bm, d), lambda i, j: (i, 0))
    gain = pl.BlockSpec((1, d), lambda i, j: (0, 0))
    in_specs = ([pl.BlockSpec((bm, kp), lambda i, j: (i, 0)) for _ in mixes]
                + [pl.BlockSpec((kp, bn), lambda i, j, p=p: (p, j)) for p in range(n_parts)]
                + [pl.BlockSpec((bm, bn), lambda i, j: (i, j)), gain] + ([gain] if with_next else []))
    operands = list(mixes) + [w16] * n_parts + [x, g_post.reshape(1, d)]
    out_specs, out_shape = [row_block], [jax.ShapeDtypeStruct((m, d), F32)]
    if with_next:
        operands.append(g_next.reshape(1, d))
        out_specs.append(row_block)
        out_shape.append(jax.ShapeDtypeStruct((m, d), BF16))
    outs = pl.pallas_call(
        functools.partial(_out_proj_norm_kernel, n_parts=n_parts, bn=bn, with_next=with_next),
        grid=(m // bm, d // bn),
        in_specs=in_specs,
        out_specs=out_specs,
        out_shape=out_shape,
        scratch_shapes=[pltpu.VMEM((bm, d), F32)],
        compiler_params=_params(2),
        name=name,
    )(*operands)
    return (outs[0], outs[1]) if with_next else (outs[0], None)


def _matmul_kernel(*refs, n_parts, w_transposed):
    x_refs, w_refs = refs[:n_parts], refs[n_parts:2 * n_parts]
    o_ref = refs[2 * n_parts]
    w16_refs = refs[2 * n_parts + 1:]
    dot = _dot_nt if w_transposed else _dot

    @pl.when(pl.program_id(1) == 0)
    def _round_weights():
        for w_ref, w16_ref in zip(w_refs, w16_refs):
            w16_ref[...] = w_ref[...].astype(BF16)

    acc = dot(x_refs[0][...], w16_refs[0][...])
    for x_ref, w16_ref in zip(x_refs[1:], w16_refs[1:]):
        acc = acc + dot(x_ref[...], w16_ref[...])
    o_ref[...] = acc.astype(o_ref.dtype)


def _matmul(xs, w, out_dtype, *, bm, bn, n_col_blocks=None, w_transposed=False, name="matmul"):
    m = xs[0].shape[0]
    kp = xs[0].shape[1]
    n_parts = len(xs)
    k_axis, n_axis = (1, 0) if w_transposed else (0, 1)
    assert all(x.shape == (m, kp) for x in xs) and w.shape[k_axis] == n_parts * kp
    if n_col_blocks is None:
        n_col_blocks = w.shape[n_axis] // bn
    w_block = (bn, kp) if w_transposed else (kp, bn)

    def w_spec(p):
        return pl.BlockSpec(w_block, (lambda j, i: (j, p)) if w_transposed else (lambda j, i: (p, j)))

    return pl.pallas_call(
        functools.partial(_matmul_kernel, n_parts=n_parts, w_transposed=w_transposed),
        grid=(n_col_blocks, m // bm),
        in_specs=[pl.BlockSpec((bm, kp), lambda j, i: (i, 0)) for _ in xs]
                 + [w_spec(p) for p in range(n_parts)],
        out_specs=pl.BlockSpec((bm, bn), lambda j, i: (i, j)),
        out_shape=jax.ShapeDtypeStruct((m, n_col_blocks * bn), out_dtype),
        scratch_shapes=[pltpu.VMEM(w_block, BF16) for _ in xs],
        compiler_params=_params(2),
        name=name,
    )(*xs, *([w] * n_parts))


def _forget_proj_kernel(x_ref, wt_ref, af_ref, aft_ref):
    w16 = wt_ref[...].astype(BF16)
    x = x_ref[...]
    af_ref[...] = _dot_nt(x, w16)
    aft_ref[...] = _dot_nt(w16, x)


def _forget_proj(x, w_t, row0, n_rows, bm=1024):
    m, k = x.shape
    assert row0 % n_rows == 0 and w_t.shape[1] == k
    return pl.pallas_call(
        _forget_proj_kernel,
        grid=(m // bm,),
        in_specs=[pl.BlockSpec((bm, k), lambda i: (i, 0)),
                  pl.BlockSpec((n_rows, k), lambda i: (row0 // n_rows, 0))],
        out_specs=[pl.BlockSpec((bm, n_rows), lambda i: (i, 0)),
                   pl.BlockSpec((n_rows, bm), lambda i: (0, i))],
        out_shape=[jax.ShapeDtypeStruct((m, n_rows), F32),
                   jax.ShapeDtypeStruct((n_rows, m), F32)],
        compiler_params=_params(1),
        name="forget_proj",
    )(x, w_t)


def _forget_cumsum_kernel(af_ref, aft_ref, bcol_ref, brow_ref, ccol_ref, crow_ref, *, tc):
    s_len, nh = af_ref.shape
    r = lax.broadcasted_iota(jnp.int32, (tc, tc), 0)
    c = lax.broadcasted_iota(jnp.int32, (tc, tc), 1)
    lower = (c <= r).astype(F32).astype(BF16)
    upper = (r <= c).astype(F32).astype(BF16)

    def body(i, carry):
        carry_row, carry_col = carry
        t0 = pl.multiple_of(i * tc, tc)
        lf = _log_sigmoid(af_ref[pl.ds(t0, tc), :] + brow_ref[...])
        lft = _log_sigmoid(aft_ref[:, pl.ds(t0, tc)] + bcol_ref[...])
        a, b, d = _split3(lf)
        cc = (_dot(lower, a) + _dot(lower, b) + _dot(lower, d)) + carry_row
        a, b, d = _split3(lft)
        cr = (_dot(a, upper) + _dot(b, upper) + _dot(d, upper)) + carry_col
        ccol_ref[pl.ds(t0, tc), :] = cc * LOG2E
        crow_ref[:, pl.ds(t0, tc)] = cr * LOG2E
        return cc[tc - 1:tc, :], cr[:, tc - 1:tc]

    lax.fori_loop(0, s_len // tc, body,
                  (jnp.zeros((1, nh), F32), jnp.zeros((nh, 1), F32)))


def _forget_cumsum(af, aft, b_f, bsz, tc=256):
    m, nh = af.shape
    s_len = m // bsz
    return pl.pallas_call(
        functools.partial(_forget_cumsum_kernel, tc=tc),
        grid=(bsz,),
        in_specs=[pl.BlockSpec((s_len, nh), lambda b: (b, 0)),
                  pl.BlockSpec((nh, s_len), lambda b: (0, b)),
                  pl.BlockSpec((nh, 1), lambda b: (0, 0)),
                  pl.BlockSpec((1, nh), lambda b: (0, 0))],
        out_specs=[pl.BlockSpec((s_len, nh), lambda b: (b, 0)),
                   pl.BlockSpec((nh, s_len), lambda b: (0, b))],
        out_shape=[jax.ShapeDtypeStruct((m, nh), F32),
                   jax.ShapeDtypeStruct((nh, m), F32)],
        compiler_params=_params(1),
        name="forget_cumsum",
    )(af, aft, b_f.reshape(nh, 1), b_f.reshape(1, nh))


def _fox_kernel(q_ref, k_ref, v_ref, g_ref, crow_ref, ccol_ref, o_ref, vone_ref, *, tq, tk, scale2, hp):
    h0 = pl.program_id(1) * hp
    s_len = q_ref.shape[0]
    lane = lax.broadcasted_iota(jnp.int32, (tq, ccol_ref.shape[-1]), 1)
    heads = [slice(i * HEAD_DIM, (i + 1) * HEAD_DIM) for i in range(hp)]
    wide = [slice(2 * i * HEAD_DIM, 2 * (i + 1) * HEAD_DIM) for i in range(hp)]
    for hs, ws in zip(heads, wide):
        vone_ref[:, ws] = jnp.concatenate(
            [v_ref[:, hs], jnp.ones((s_len, HEAD_DIM), vone_ref.dtype)], axis=1)

    def q_body(qi, _):
        q0 = pl.multiple_of(qi * tq, tq)
        qs = [q_ref[pl.ds(q0, tq), hs] for hs in heads]
        c_blk = ccol_ref[pl.ds(q0, tq), :]
        c_ts = [jnp.sum(jnp.where(lane == h0 + i, c_blk, 0.0), axis=1, keepdims=True) for i in range(hp)]
        n_full = q0 // tk

        def logits(k0, width):
            return tuple(_dot_nt(qs[i], k_ref[pl.ds(k0, width), hs]) * scale2 - crow_ref[i, :, pl.ds(k0, width)]
                         for i, hs in enumerate(heads))

        def absorb(us, k0, width, state, diag_offset=None):
            if diag_offset is not None:
                keep = (lax.broadcasted_iota(jnp.int32, (tq, width), 1)
                        <= lax.broadcasted_iota(jnp.int32, (tq, width), 0) + diag_offset)
                us = [jnp.where(keep, u, NEG) for u in us]
            m_news = [jnp.maximum(st[0], jnp.max(u, axis=1, keepdims=True) + c_t)
                      for u, st, c_t in zip(us, state, c_ts)]
            ps = [jnp.exp2(u - (m_new - c_t)).astype(BF16) for u, m_new, c_t in zip(us, m_news, c_ts)]
            pvs = [_dot(p, vone_ref[pl.ds(k0, width), ws]) for p, ws in zip(ps, wide)]
            return tuple((m_new, jnp.exp2(m - m_new) * acc + pv)
                         for (m, acc), m_new, pv in zip(state, m_news, pvs))

        def absorb_tiles(tiles, state):
            all_us = [logits(k0, width) for k0, width, _ in tiles]
            for us, (k0, width, diag_offset) in zip(all_us, tiles):
                state = absorb(us, k0, width, state, diag_offset)
            return state

        def step2(kp, state):
            k0 = pl.multiple_of(2 * kp * tk, tk)
            return absorb_tiles([(k0, tk, None), (k0 + tk, tk, None)], state)

        state = tuple((jnp.full((tq, 1), NEG, F32), jnp.zeros((tq, 2 * HEAD_DIM), F32)) for _ in heads)
        state = lax.fori_loop(0, n_full // 2, step2, state)
        k_diag = pl.multiple_of(n_full * tk, tk)
        n_r = tk // tq

        def last_branch(odd, r):
            width = (r + 1) * tq

            def run():
                gates = [_silu(g_ref[pl.ds(q0, tq), hs].astype(F32)) for hs in heads]
                tiles = [(pl.multiple_of(k_diag - tk, tk), tk, None)] if odd else []
                final = absorb_tiles(tiles + [(k_diag, width, r * tq)], state)
                for (_, acc), gate, hs in zip(final, gates, heads):
                    o_ref[pl.ds(q0, tq), hs] = ((acc[:, :HEAD_DIM] / acc[:, HEAD_DIM:]) * gate).astype(o_ref.dtype)
                return 0

            return run

        return lax.switch((n_full % 2) * n_r + qi % n_r,
                          [last_branch(odd, r) for odd in (0, 1) for r in range(n_r)])

    lax.fori_loop(0, s_len // tq, q_body, 0)


def _fox_attention(proj, c_row, c_col, bsz, s_len, n_heads, col_blocks, tq=256, tk=1024, hp=2):
    qb, kb, vb, gb = col_blocks
    scale2 = LOG2E * HEAD_DIM ** -0.5
    assert n_heads % hp == 0 and all(off % hp == 0 for off in col_blocks)
    assert tk % tq == 0 and s_len % tk == 0

    def head_spec(off):
        return pl.BlockSpec((s_len, hp * HEAD_DIM), lambda b, h: (b, off // hp + h))

    return pl.pallas_call(
        functools.partial(_fox_kernel, tq=tq, tk=tk, scale2=scale2, hp=hp),
        grid=(bsz, n_heads // hp),
        in_specs=[head_spec(qb), head_spec(kb), head_spec(vb), head_spec(gb),
                  pl.BlockSpec((hp, 1, s_len), lambda b, h: (h, 0, b)),
                  pl.BlockSpec((s_len, n_heads), lambda b, h: (b, 0))],
        out_specs=pl.BlockSpec((s_len, hp * HEAD_DIM), lambda b, h: (b, h)),
        out_shape=jax.ShapeDtypeStruct((bsz * s_len, n_heads * HEAD_DIM), BF16),
        scratch_shapes=[pltpu.VMEM((s_len, hp * 2 * HEAD_DIM), BF16)],
        compiler_params=_params(2),
        name="fox_attention",
    )(proj, proj, proj, proj, c_row.reshape(n_heads, 1, bsz * s_len), c_col)


def _chunk_kernel(q_ref, k_ref, v_ref, g_ref, rel_ref, o_ref, kpad, vpad, bias_ref, *, tq, scale, hp):
    b = pl.program_id(1)
    s_len = q_ref.shape[0]
    pad = LEFT_CHUNKS * CHUNK
    win = tq + pad
    rp = rel_ref.shape[-1]
    heads = [slice(i * HEAD_DIM, (i + 1) * HEAD_DIM) for i in range(hp)]

    @pl.when(b == 0)
    def _build_bias():
        u = lax.broadcasted_iota(jnp.int32, (rp, win), 1)
        r = lax.broadcasted_iota(jnp.int32, (rp, win), 0)
        idx = jnp.where(u < pad + CHUNK, jnp.clip(pad - u, -REL_CLIP, REL_CLIP) + REL_CLIP, 2 * REL_CLIP)
        sel = (r == idx).astype(F32).astype(BF16)
        qc = lax.broadcasted_iota(jnp.int32, (tq, win), 0) // CHUNK
        kc = lax.broadcasted_iota(jnp.int32, (tq, win), 1) // CHUNK
        in_band = (kc >= qc) & (kc <= qc + LEFT_CHUNKS)
        for i in range(hp):
            a, bb, d = _split3(jnp.broadcast_to(rel_ref[i], (8, rp)))
            g = _dot(a, sel) + _dot(bb, sel) + _dot(d, sel)
            tile = pltpu.roll(jnp.broadcast_to(g[0:1, :], (tq, win)), 0, 1, stride=1, stride_axis=0)
            bias_ref[i] = jnp.where(in_band, tile, NEG)

    kpad[pl.ds(0, pad), :] = jnp.zeros((pad, hp * HEAD_DIM), kpad.dtype)
    kpad[pl.ds(pad, s_len), :] = k_ref[...]
    wide = [slice(2 * i * HEAD_DIM, 2 * (i + 1) * HEAD_DIM) for i in range(hp)]
    vpad[pl.ds(0, pad), :] = jnp.zeros((pad, 2 * hp * HEAD_DIM), vpad.dtype)
    for hs, ws in zip(heads, wide):
        vpad[pl.ds(pad, s_len), ws] = jnp.concatenate(
            [v_ref[:, hs], jnp.ones((s_len, HEAD_DIM), vpad.dtype)], axis=1)
    key_pos = lax.broadcasted_iota(jnp.int32, (tq, win), 1)

    def body(j, _, before_start):
        q0 = pl.multiple_of(j * tq, tq)
        ss = [_dot_nt(q_ref[pl.ds(q0, tq), hs], kpad[pl.ds(q0, win), hs]) * scale + bias_ref[i]
              for i, hs in enumerate(heads)]
        if before_start:
            ss = [jnp.where(key_pos + q0 >= pad, s, NEG) for s in ss]
        ps = [jnp.exp(s - jnp.max(s, axis=1, keepdims=True)).astype(BF16) for s in ss]
        pvs = [_dot(p, vpad[pl.ds(q0, win), ws]) for p, ws in zip(ps, wide)]
        for pv, hs in zip(pvs, heads):
            o = pv[:, :HEAD_DIM] / pv[:, HEAD_DIM:]
            g = g_ref[pl.ds(q0, tq), hs].astype(F32)
            o_ref[pl.ds(q0, tq), hs] = (o * _silu(g)).astype(o_ref.dtype)
        return 0

    n_start = pad // tq
    lax.fori_loop(0, n_start, functools.partial(body, before_start=True), 0)
    lax.fori_loop(n_start, s_len // tq, functools.partial(body, before_start=False), 0)


def _chunk_attention(proj, rel_bias, bsz, s_len, n_heads, col_blocks, tq=256, hp=4):
    qb, kb, vb, gb = col_blocks
    scale = HEAD_DIM ** -0.5
    pad = LEFT_CHUNKS * CHUNK
    rp = 3 * LANES
    assert N_REL <= rp and pad % tq == 0
    assert n_heads % hp == 0 and all(off % hp == 0 for off in col_blocks)
    rel = jnp.pad(rel_bias, ((0, 0), (0, rp - N_REL))).reshape(n_heads, 1, rp)

    def head_spec(off):
        return pl.BlockSpec((s_len, hp * HEAD_DIM), lambda h, b: (b, off // hp + h))

    return pl.pallas_call(
        functools.partial(_chunk_kernel, tq=tq, scale=scale, hp=hp),
        grid=(n_heads // hp, bsz),
        in_specs=[head_spec(qb), head_spec(kb), head_spec(vb), head_spec(gb),
                  pl.BlockSpec((hp, 1, rp), lambda h, b: (h, 0, 0))],
        out_specs=pl.BlockSpec((s_len, hp * HEAD_DIM), lambda h, b: (b, h)),
        out_shape=jax.ShapeDtypeStruct((bsz * s_len, n_heads * HEAD_DIM), BF16),
        scratch_shapes=[pltpu.VMEM((s_len + pad, hp * HEAD_DIM), BF16),
                        pltpu.VMEM((s_len + pad, 2 * hp * HEAD_DIM), BF16),
                        pltpu.VMEM((hp, tq, tq + pad), F32)],
        compiler_params=_params(2),
        name="chunk_attention",
    )(proj, proj, proj, proj, rel)


def _sb_kernel(q_ref, k_ref, v_ref, g_ref, o_ref, *, ts, tl, rows_per_iter, scale2, hp):
    s_len = q_ref.shape[0]
    heads = [slice(i * HEAD_DIM, (i + 1) * HEAD_DIM) for i in range(hp)]

    def after2(n):
        a = lax.broadcasted_iota(jnp.int32, (n, n), 0) > lax.broadcasted_iota(jnp.int32, (n, n), 1)
        a = a.astype(F32).astype(BF16)
        return jnp.concatenate([a, a], axis=0)

    tri = {n: after2(n) for n in {ts, tl}}
    strict = lax.broadcasted_iota(jnp.int32, (ts, ts), 1) < lax.broadcasted_iota(jnp.int32, (ts, ts), 0)

    def log2_terms(q, k):
        z = _dot_nt(q, k) * scale2
        lb = jnp.minimum(z, 0.0) - jnp.log2(1.0 + jnp.exp2(-jnp.abs(z)))
        return lb, lb - z

    def suffix_sums(lom):
        hi = lom.astype(BF16)
        lo = (lom - hi.astype(F32)).astype(BF16)
        tail = _dot(jnp.concatenate([hi, lo], axis=1), tri[lom.shape[1]])
        return tail, tail[:, 0:1] + lom[:, 0:1]

    def first_visit(chains):
        rows = [pl.ds(r0, ts) for _, r0, _ in chains]
        keys = [pl.ds(r0 - lw, lw + ts) for _, r0, lw in chains]
        qs = [q_ref[rs, hs] for rs, (hs, _, _) in zip(rows, chains)]
        terms = [log2_terms(q, k_ref[ks, hs]) for q, ks, (hs, _, _) in zip(qs, keys, chains)]
        sums = []
        for (lb, lom), (_, _, lw) in zip(terms, chains):
            tail_d, rsum = suffix_sums(jnp.where(strict, lom[:, lw:], 0.0))
            tail_l = None
            if lw:
                tail_l, rsum_l = suffix_sums(lom[:, :lw])
                tail_l, rsum = tail_l + rsum, rsum + rsum_l
            sums.append((tail_l, tail_d, rsum))
        accs = []
        for (lb, _), (tail_l, tail_d, _), ks, (hs, _, lw) in zip(terms, sums, keys, chains):
            a = jnp.where(strict, jnp.exp2(lb[:, lw:] + tail_d), 0.0)
            if lw:
                a = jnp.concatenate([jnp.exp2(lb[:, :lw] + tail_l), a], axis=1)
            accs.append(_dot(a.astype(BF16), v_ref[ks, hs]))
        gates = [_silu(g_ref[rs, hs].astype(F32)) for rs, (hs, _, _) in zip(rows, chains)]
        for rs, (hs, _, _), acc, gate in zip(rows, chains, accs, gates):
            o_ref[rs, hs] = (acc * gate).astype(o_ref.dtype)
        return [(q, s[2], acc, gate) for q, s, acc, gate in zip(qs, sums, accs, gates)]

    def more(carry):
        k0, rsum, _ = carry
        return jnp.logical_and(k0 >= 0, jnp.max(rsum) >= SB_STOP_LOG2)

    def visit(chains):
        carries = first_visit(chains)

        worst = functools.reduce(jnp.maximum, [c[1] for c in carries])

        @pl.when(jnp.max(worst) >= SB_STOP_LOG2)
        def _visit_more_keys():
            for (hs, r0, lw), (q, rsum, acc, gate) in zip(chains, carries):
                def tile(carry, q=q, hs=hs):
                    k0, rsum, acc = carry
                    ks = pl.ds(pl.multiple_of(k0, ts), ts)
                    lb, lom = log2_terms(q, k_ref[ks, hs])
                    tail, rs = suffix_sums(lom)
                    a = jnp.exp2(lb + tail + rsum)
                    return k0 - ts, rsum + rs, acc + _dot(a.astype(BF16), v_ref[ks, hs])

                k_next = jnp.asarray(r0 - lw - ts, jnp.int32)
                acc = lax.while_loop(more, tile, (k_next, rsum, acc))[2]
                o_ref[pl.ds(r0, ts), hs] = (acc * gate).astype(o_ref.dtype)

    visit([(hs, r0, min(r0, tl)) for r0 in range(0, rows_per_iter, ts) for hs in heads])

    def q_body(it, _):
        q0 = pl.multiple_of((it + 1) * rows_per_iter, rows_per_iter)
        visit([(hs, q0 + off, tl) for off in range(0, rows_per_iter, ts) for hs in heads])
        return 0

    lax.fori_loop(0, s_len // rows_per_iter - 1, q_body, 0)


def _sb_attention(proj, bsz, s_len, n_heads, col_blocks, ts=128, tl=256, rows_per_iter=512, hp=4):
    qb, kb, vb, gb = col_blocks
    scale2 = LOG2E * HEAD_DIM ** -0.5
    assert n_heads % hp == 0 and all(off % hp == 0 for off in col_blocks)
    assert tl % ts == 0 and rows_per_iter % ts == 0 and s_len % rows_per_iter == 0

    def head_spec(off):
        return pl.BlockSpec((s_len, hp * HEAD_DIM), lambda b, h: (b, off // hp + h))

    return pl.pallas_call(
        functools.partial(_sb_kernel, ts=ts, tl=tl, rows_per_iter=rows_per_iter, scale2=scale2, hp=hp),
        grid=(bsz, n_heads // hp),
        in_specs=[head_spec(qb), head_spec(kb), head_spec(vb), head_spec(gb)],
        out_specs=pl.BlockSpec((s_len, hp * HEAD_DIM), lambda b, h: (b, h)),
        out_shape=jax.ShapeDtypeStruct((bsz * s_len, n_heads * HEAD_DIM), BF16),
        compiler_params=_params(2),
        name="sb_attention",
    )(proj, proj, proj, proj)


def _even_layer(x2, h, bsz, s_len, g_post, g_next, w_in, b_f, rel_bias, w_out):
    n_a = b_f.shape[0]
    n_b = rel_bias.shape[0]
    main_cols = 4 * (n_a + n_b) * HEAD_DIM
    w_in_t = w_in.T
    proj = _matmul([h], w_in_t, BF16, bm=PROJ_BM, bn=PROJ_BN, n_col_blocks=main_cols // PROJ_BN,
                   w_transposed=True, name="in_proj_even")
    af, aft = _forget_proj(h, w_in_t, main_cols, n_a)
    c_col, c_row = _forget_cumsum(af, aft, b_f, bsz)
    mix_a = _fox_attention(proj, c_row, c_col, bsz, s_len, n_a,
                           (0, n_a, 2 * n_a, 3 * n_a))
    off = 4 * n_a
    mix_b = _chunk_attention(proj, rel_bias, bsz, s_len, n_b,
                             (off, off + n_b, off + 2 * n_b, off + 3 * n_b))
    return _out_proj_norm([mix_a, mix_b], w_out.astype(BF16), x2, g_post, g_next,
                          bm=OUT_BM, bn=OUT_BN, name="out_proj_norm_even")


def _odd_layer(x2, h, bsz, s_len, g_post, g_next, w_in, w_out):
    n_c = w_out.shape[0] // HEAD_DIM
    proj = _matmul([h], w_in, BF16, bm=PROJ_BM, bn=PROJ_BN, name="in_proj_odd")
    mix = _sb_attention(proj, bsz, s_len, n_c, (0, n_c, 2 * n_c, 3 * n_c))
    return _out_proj_norm([mix], w_out.astype(BF16), x2, g_post, g_next,
                          bm=OUT_BM, bn=OUT_BN, name="out_proj_norm_odd")


def kernel(x, norm_pre, norm_post, w_in_even, b_f_even, rel_bias_even, w_out_even, w_in_odd, w_out_odd):
    bsz, s_len, d = x.shape
    depth = norm_pre.shape[0]
    x2 = x.reshape(bsz * s_len, d)
    h = _rmsnorm(x2, norm_pre[0], BF16)
    for layer in range(depth):
        i = layer // 2
        g_next = norm_pre[layer + 1] if layer + 1 < depth else None
        if layer % 2 == 0:
            x2, h = _even_layer(x2, h, bsz, s_len, norm_post[layer], g_next,
                                w_in_even[i], b_f_even[i], rel_bias_even[i], w_out_even[i])
        else:
            x2, h = _odd_layer(x2, h, bsz, s_len, norm_post[layer], g_next,
                               w_in_odd[i], w_out_odd[i])
    return x2.reshape(bsz, s_len, d)
```

```python
import functools

import jax
import jax.numpy as jnp
from jax import lax
from jax.experimental import pallas as pl
from jax.experimental.pallas import tpu as pltpu

HEAD_DIM = 128
CHUNK = 64
LEFT_CHUNKS = 8
REL_CLIP = 128
N_REL = 2 * REL_CLIP + 1
RMS_EPS = 1e-6

LANES = 128
VMEM_LIMIT = 60 * 1024 * 1024
PROJ_BM, PROJ_BN = 512, 1024
OUT_BM, OUT_BN = 512, 512
NORM_ROWS = 128
NEG = -1e30
LOG2E = 1.4426950408889634
SB_STOP_LOG2 = -152.0

BF16 = jnp.bfloat16
F32 = jnp.float32


def _params(n_axes):
    return pltpu.CompilerParams(dimension_semantics=("arbitrary",) * n_axes,
                                vmem_limit_bytes=VMEM_LIMIT)


def _dot_nt(a, b):
    return lax.dot_general(a, b, (((1,), (1,)), ((), ())), preferred_element_type=F32)


def _dot(a, b):
    return jnp.dot(a, b, preferred_element_type=F32)


def _split3(x):
    hi = x.astype(BF16)
    r1 = x - hi.astype(F32)
    mid = r1.astype(BF16)
    lo = (r1 - mid.astype(F32)).astype(BF16)
    return hi, mid, lo


def _log_sigmoid(z):
    return jnp.minimum(z, 0.0) - jnp.log1p(jnp.exp(-jnp.abs(z)))


def _silu(g):
    return g / (1.0 + jnp.exp(-g))


def _rmsnorm_kernel(x_ref, g_ref, o_ref):
    x = x_ref[...]
    ms = jnp.mean(x * x, axis=-1, keepdims=True)
    o_ref[...] = (x * lax.rsqrt(ms + RMS_EPS) * g_ref[...]).astype(o_ref.dtype)


def _rmsnorm(x, g, out_dtype, bm=256):
    m, d = x.shape
    return pl.pallas_call(
        _rmsnorm_kernel,
        grid=(m // bm,),
        in_specs=[pl.BlockSpec((bm, d), lambda i: (i, 0)),
                  pl.BlockSpec((1, d), lambda i: (0, 0))],
        out_specs=pl.BlockSpec((bm, d), lambda i: (i, 0)),
        out_shape=jax.ShapeDtypeStruct((m, d), out_dtype),
        compiler_params=_params(1),
        name="rmsnorm",
    )(x, g.reshape(1, d))


def _out_proj_norm_kernel(*refs, n_parts, bn, with_next):
    mix_refs, w_refs = refs[:n_parts], refs[n_parts:2 * n_parts]
    rest = list(refs[2 * n_parts:])
    xres_ref, gpost_ref = rest.pop(0), rest.pop(0)
    gnext_ref = rest.pop(0) if with_next else None
    o_ref = rest.pop(0)
    h_ref = rest.pop(0) if with_next else None
    xs_ref = rest.pop(0)
    j = pl.program_id(1)
    c0 = pl.multiple_of(j * bn, bn)

    y = _dot(mix_refs[0][...], w_refs[0][...])
    for mix_ref, w_ref in zip(mix_refs[1:], w_refs[1:]):
        y = y + _dot(mix_ref[...], w_ref[...])
    o_ref[:, pl.ds(c0, bn)] = y
    xs_ref[:, pl.ds(c0, bn)] = xres_ref[...]

    @pl.when(j == pl.num_programs(1) - 1)
    def _normalise():
        def rows(r, _):
            rs = pl.ds(pl.multiple_of(r * NORM_ROWS, NORM_ROWS), NORM_ROWS)
            y = o_ref[rs, :]
            ms = jnp.mean(y * y, axis=-1, keepdims=True)
            x_new = xs_ref[rs, :] + y * lax.rsqrt(ms + RMS_EPS) * gpost_ref[...]
            o_ref[rs, :] = x_new
            if with_next:
                ms = jnp.mean(x_new * x_new, axis=-1, keepdims=True)
                h_ref[rs, :] = (x_new * lax.rsqrt(ms + RMS_EPS) * gnext_ref[...]).astype(h_ref.dtype)
            return 0

        lax.fori_loop(0, o_ref.shape[0] // NORM_ROWS, rows, 0)


def _out_proj_norm(mixes, w16, x, g_post, g_next, *, bm, bn, name):
    m, d = x.shape
    kp = mixes[0].shape[1]
    n_parts = len(mixes)
    with_next = g_next is not None
    assert all(t.shape == (m, kp) for t in mixes) and w16.shape == (n_parts * kp, d)
    row_block = pl.BlockSpec((bm, d), lambda i, j: (i, 0))
    gain = pl.BlockSpec((1, d), lambda i, j: (0, 0))
    in_specs = ([pl.BlockSpec((bm, kp), lambda i, j: (i, 0)) for _ in mixes]
                + [pl.BlockSpec((kp, bn), lambda i, j, p=p: (p, j)) for p in range(n_parts)]
                + [pl.BlockSpec((bm, bn), lambda i, j: (i, j)), gain] + ([gain] if with_next else []))
    operands = list(mixes) + [w16] * n_parts + [x, g_post.reshape(1, d)]
    out_specs, out_shape = [row_block], [jax.ShapeDtypeStruct((m, d), F32)]
    if with_next:
        operands.append(g_next.reshape(1, d))
        out_specs.append(row_block)
        out_shape.append(jax.ShapeDtypeStruct((m, d), BF16))
    outs = pl.pallas_call(
        functools.partial(_out_proj_norm_kernel, n_parts=n_parts, bn=bn, with_next=with_next),
        grid=(m // bm, d // bn),
        in_specs=in_specs,
        out_specs=out_specs,
        out_shape=out_shape,
        scratch_shapes=[pltpu.VMEM((bm, d), F32)],
        compiler_params=_params(2),
        name=name,
    )(*operands)
    return (outs[0], outs[1]) if with_next else (outs[0], None)


def _in_proj_kernel(x_ref, w_ref, side_ref, o_ref, side16_ref, w16_ref, *, w_transposed):
    @pl.when(pl.program_id(1) == 0)
    def _round_weights():
        w16_ref[...] = w_ref[...].astype(BF16)

    side16_ref[...] = side_ref[...].astype(BF16)
    dot = _dot_nt if w_transposed else _dot
    o_ref[...] = dot(x_ref[...], w16_ref[...]).astype(o_ref.dtype)


def _in_proj(x, w, side, out_dtype, *, bm, bn, n_col_blocks, w_transposed=False, name="in_proj"):
    m, k = x.shape
    n_row_blocks = m // bm
    n_steps = n_col_blocks * n_row_blocks
    side_rows = side.shape[0] // n_steps
    assert w.shape[1 if w_transposed else 0] == k
    assert side_rows * n_steps == side.shape[0] and side_rows % 16 == 0
    w_block = (bn, k) if w_transposed else (k, bn)
    side_spec = pl.BlockSpec((side_rows, side.shape[1]), lambda j, i: (j * n_row_blocks + i, 0))
    return pl.pallas_call(
        functools.partial(_in_proj_kernel, w_transposed=w_transposed),
        grid=(n_col_blocks, n_row_blocks),
        in_specs=[pl.BlockSpec((bm, k), lambda j, i: (i, 0)),
                  pl.BlockSpec(w_block, (lambda j, i: (j, 0)) if w_transposed else (lambda j, i: (0, j))),
                  side_spec],
        out_specs=[pl.BlockSpec((bm, bn), lambda j, i: (i, j)), side_spec],
        out_shape=[jax.ShapeDtypeStruct((m, n_col_blocks * bn), out_dtype),
                   jax.ShapeDtypeStruct(side.shape, BF16)],
        scratch_shapes=[pltpu.VMEM(w_block, BF16)],
        compiler_params=_params(2),
        name=name,
    )(x, w, side)


def _forget_proj_kernel(x_ref, wt_ref, af_ref, aft_ref):
    w16 = wt_ref[...].astype(BF16)
    x = x_ref[...]
    af_ref[...] = _dot_nt(x, w16)
    aft_ref[...] = _dot_nt(w16, x)


def _forget_proj(x, w_t, row0, n_rows, bm=1024):
    m, k = x.shape
    assert row0 % n_rows == 0 and w_t.shape[1] == k
    return pl.pallas_call(
        _forget_proj_kernel,
        grid=(m // bm,),
        in_specs=[pl.BlockSpec((bm, k), lambda i: (i, 0)),
                  pl.BlockSpec((n_rows, k), lambda i: (row0 // n_rows, 0))],
        out_specs=[pl.BlockSpec((bm, n_rows), lambda i: (i, 0)),
                   pl.BlockSpec((n_rows, bm), lambda i: (0, i))],
        out_shape=[jax.ShapeDtypeStruct((m, n_rows), F32),
                   jax.ShapeDtypeStruct((n_rows, m), F32)],
        compiler_params=_params(1),
        name="forget_proj",
    )(x, w_t)


def _forget_cumsum_kernel(af_ref, aft_ref, bcol_ref, brow_ref, ccol_ref, crow_ref, *, tc):
    s_len, nh = af_ref.shape
    r = lax.broadcasted_iota(jnp.int32, (tc, tc), 0)
    c = lax.broadcasted_iota(jnp.int32, (tc, tc), 1)
    lower = (c <= r).astype(F32).astype(BF16)
    upper = (r <= c).astype(F32).astype(BF16)

    def body(i, carry):
        carry_row, carry_col = carry
        t0 = pl.multiple_of(i * tc, tc)
        lf = _log_sigmoid(af_ref[pl.ds(t0, tc), :] + brow_ref[...])
        lft = _log_sigmoid(aft_ref[:, pl.ds(t0, tc)] + bcol_ref[...])
        a, b, d = _split3(lf)
        cc = (_dot(lower, a) + _dot(lower, b) + _dot(lower, d)) + carry_row
        a, b, d = _split3(lft)
        cr = (_dot(a, upper) + _dot(b, upper) + _dot(d, upper)) + carry_col
        ccol_ref[pl.ds(t0, tc), :] = cc * LOG2E
        crow_ref[:, pl.ds(t0, tc)] = cr * LOG2E
        return cc[tc - 1:tc, :], cr[:, tc - 1:tc]

    lax.fori_loop(0, s_len // tc, body,
                  (jnp.zeros((1, nh), F32), jnp.zeros((nh, 1), F32)))


def _forget_cumsum(af, aft, b_f, bsz, tc=256):
    m, nh = af.shape
    s_len = m // bsz
    return pl.pallas_call(
        functools.partial(_forget_cumsum_kernel, tc=tc),
        grid=(bsz,),
        in_specs=[pl.BlockSpec((s_len, nh), lambda b: (b, 0)),
                  pl.BlockSpec((nh, s_len), lambda b: (0, b)),
                  pl.BlockSpec((nh, 1), lambda b: (0, 0)),
                  pl.BlockSpec((1, nh), lambda b: (0, 0))],
        out_specs=[pl.BlockSpec((s_len, nh), lambda b: (b, 0)),
                   pl.BlockSpec((nh, s_len), lambda b: (0, b))],
        out_shape=[jax.ShapeDtypeStruct((m, nh), F32),
                   jax.ShapeDtypeStruct((nh, m), F32)],
        compiler_params=_params(1),
        name="forget_cumsum",
    )(af, aft, b_f.reshape(nh, 1), b_f.reshape(1, nh))


def _fox_kernel(q_ref, k_ref, v_ref, g_ref, crow_ref, ccol_ref, o_ref, vone_ref, *, tq, tk, scale2, hp):
    h0 = pl.program_id(1) * hp
    s_len = q_ref.shape[0]
    lane = lax.broadcasted_iota(jnp.int32, (tq, ccol_ref.shape[-1]), 1)
    heads = [slice(i * HEAD_DIM, (i + 1) * HEAD_DIM) for i in range(hp)]
    wide = [slice(2 * i * HEAD_DIM, 2 * (i + 1) * HEAD_DIM) for i in range(hp)]
    for hs, ws in zip(heads, wide):
        vone_ref[:, ws] = jnp.concatenate(
            [v_ref[:, hs], jnp.ones((s_len, HEAD_DIM), vone_ref.dtype)], axis=1)

    def q_body(qi, _):
        q0 = pl.multiple_of(qi * tq, tq)
        qs = [q_ref[pl.ds(q0, tq), hs] for hs in heads]
        c_blk = ccol_ref[pl.ds(q0, tq), :]
        c_ts = [jnp.sum(jnp.where(lane == h0 + i, c_blk, 0.0), axis=1, keepdims=True) for i in range(hp)]
        n_full = q0 // tk

        def logits(k0, width):
            return tuple(_dot_nt(qs[i], k_ref[pl.ds(k0, width), hs]) * scale2 - crow_ref[i, :, pl.ds(k0, width)]
                         for i, hs in enumerate(heads))

        def absorb(us, k0, width, state, diag_offset=None):
            if diag_offset is not None:
                keep = (lax.broadcasted_iota(jnp.int32, (tq, width), 1)
                        <= lax.broadcasted_iota(jnp.int32, (tq, width), 0) + diag_offset)
                us = [jnp.where(keep, u, NEG) for u in us]
            m_news = [jnp.maximum(st[0], jnp.max(u, axis=1, keepdims=True) + c_t)
                      for u, st, c_t in zip(us, state, c_ts)]
            ps = [jnp.exp2(u - (m_new - c_t)).astype(BF16) for u, m_new, c_t in zip(us, m_news, c_ts)]
            pvs = [_dot(p, vone_ref[pl.ds(k0, width), ws]) for p, ws in zip(ps, wide)]
            return tuple((m_new, jnp.exp2(m - m_new) * acc + pv)
                         for (m, acc), m_new, pv in zip(state, m_news, pvs))

        def absorb_tiles(tiles, state):
            all_us = [logits(k0, width) for k0, width, _ in tiles]
            for us, (k0, width, diag_offset) in zip(all_us, tiles):
                state = absorb(us, k0, width, state, diag_offset)
            return state

        def step2(kp, state):
            k0 = pl.multiple_of(2 * kp * tk, tk)
            return absorb_tiles([(k0, tk, None), (k0 + tk, tk, None)], state)

        state = tuple((jnp.full((tq, 1), NEG, F32), jnp.zeros((tq, 2 * HEAD_DIM), F32)) for _ in heads)
        state = lax.fori_loop(0, n_full // 2, step2, state)
        k_diag = pl.multiple_of(n_full * tk, tk)
        n_r = tk // tq

        def last_branch(odd, r):
            width = (r + 1) * tq

            def run():
                gates = [_silu(g_ref[pl.ds(q0, tq), hs].astype(F32)) for hs in heads]
                tiles = [(pl.multiple_of(k_diag - tk, tk), tk, None)] if odd else []
                final = absorb_tiles(tiles + [(k_diag, width, r * tq)], state)
                for (_, acc), gate, hs in zip(final, gates, heads):
                    o_ref[pl.ds(q0, tq), hs] = ((acc[:, :HEAD_DIM] / acc[:, HEAD_DIM:]) * gate).astype(o_ref.dtype)
                return 0

            return run

        return lax.switch((n_full % 2) * n_r + qi % n_r,
                          [last_branch(odd, r) for odd in (0, 1) for r in range(n_r)])

    lax.fori_loop(0, s_len // tq, q_body, 0)


def _fox_attention(proj, c_row, c_col, bsz, s_len, n_heads, col_blocks, tq=256, tk=1024, hp=2):
    qb, kb, vb, gb = col_blocks
    scale2 = LOG2E * HEAD_DIM ** -0.5
    assert n_heads % hp == 0 and all(off % hp == 0 for off in col_blocks)
    assert tk % tq == 0 and s_len % tk == 0

    def head_spec(off):
        return pl.BlockSpec((s_len, hp * HEAD_DIM), lambda b, h: (b, off // hp + h))

    return pl.pallas_call(
        functools.partial(_fox_kernel, tq=tq, tk=tk, scale2=scale2, hp=hp),
        grid=(bsz, n_heads // hp),
        in_specs=[head_spec(qb), head_spec(kb), head_spec(vb), head_spec(gb),
                  pl.BlockSpec((hp, 1, s_len), lambda b, h: (h, 0, b)),
                  pl.BlockSpec((s_len, n_heads), lambda b, h: (b, 0))],
        out_specs=pl.BlockSpec((s_len, hp * HEAD_DIM), lambda b, h: (b, h)),
        out_shape=jax.ShapeDtypeStruct((bsz * s_len, n_heads * HEAD_DIM), BF16),
        scratch_shapes=[pltpu.VMEM((s_len, hp * 2 * HEAD_DIM), BF16)],
        compiler_params=_params(2),
        name="fox_attention",
    )(proj, proj, proj, proj, c_row.reshape(n_heads, 1, bsz * s_len), c_col)


def _chunk_kernel(q_ref, k_ref, v_ref, g_ref, rel_ref, o_ref, kpad, vpad, bias_ref, *, tq, scale, hp):
    b = pl.program_id(1)
    s_len = q_ref.shape[0]
    pad = LEFT_CHUNKS * CHUNK
    win = tq + pad
    rp = rel_ref.shape[-1]
    heads = [slice(i * HEAD_DIM, (i + 1) * HEAD_DIM) for i in range(hp)]

    @pl.when(b == 0)
    def _build_bias():
        u = lax.broadcasted_iota(jnp.int32, (rp, win), 1)
        r = lax.broadcasted_iota(jnp.int32, (rp, win), 0)
        idx = jnp.where(u < pad + CHUNK, jnp.clip(pad - u, -REL_CLIP, REL_CLIP) + REL_CLIP, 2 * REL_CLIP)
        sel = (r == idx).astype(F32).astype(BF16)
        qc = lax.broadcasted_iota(jnp.int32, (tq, win), 0) // CHUNK
        kc = lax.broadcasted_iota(jnp.int32, (tq, win), 1) // CHUNK
        in_band = (kc >= qc) & (kc <= qc + LEFT_CHUNKS)
        for i in range(hp):
            a, bb, d = _split3(jnp.broadcast_to(rel_ref[i], (8, rp)))
            g = _dot(a, sel) + _dot(bb, sel) + _dot(d, sel)
            tile = pltpu.roll(jnp.broadcast_to(g[0:1, :], (tq, win)), 0, 1, stride=1, stride_axis=0)
            bias_ref[i] = jnp.where(in_band, tile, NEG)

    kpad[pl.ds(0, pad), :] = jnp.zeros((pad, hp * HEAD_DIM), kpad.dtype)
    kpad[pl.ds(pad, s_len), :] = k_ref[...]
    wide = [slice(2 * i * HEAD_DIM, 2 * (i + 1) * HEAD_DIM) for i in range(hp)]
    vpad[pl.ds(0, pad), :] = jnp.zeros((pad, 2 * hp * HEAD_DIM), vpad.dtype)
    for hs, ws in zip(heads, wide):
        vpad[pl.ds(pad, s_len), ws] = jnp.concatenate(
            [v_ref[:, hs], jnp.ones((s_len, HEAD_DIM), vpad.dtype)], axis=1)
    key_pos = lax.broadcasted_iota(jnp.int32, (tq, win), 1)

    def body(j, _, before_start):
        q0 = pl.multiple_of(j * tq, tq)
        ss = [_dot_nt(q_ref[pl.ds(q0, tq), hs], kpad[pl.ds(q0, win), hs]) * scale + bias_ref[i]
              for i, hs in enumerate(heads)]
        if before_start:
            ss = [jnp.where(key_pos + q0 >= pad, s, NEG) for s in ss]
        ps = [jnp.exp(s - jnp.max(s, axis=1, keepdims=True)).astype(BF16) for s in ss]
        pvs = [_dot(p, vpad[pl.ds(q0, win), ws]) for p, ws in zip(ps, wide)]
        for pv, hs in zip(pvs, heads):
            o = pv[:, :HEAD_DIM] / pv[:, HEAD_DIM:]
            g = g_ref[pl.ds(q0, tq), hs].astype(F32)
            o_ref[pl.ds(q0, tq), hs] = (o * _silu(g)).astype(o_ref.dtype)
        return 0

    n_start = pad // tq
    lax.fori_loop(0, n_start, functools.partial(body, before_start=True), 0)
    lax.fori_loop(n_start, s_len // tq, functools.partial(body, before_start=False), 0)


def _chunk_attention(proj, rel_bias, bsz, s_len, n_heads, col_blocks, tq=256, hp=4):
    qb, kb, vb, gb = col_blocks
    scale = HEAD_DIM ** -0.5
    pad = LEFT_CHUNKS * CHUNK
    rp = 3 * LANES
    assert N_REL <= rp and pad % tq == 0
    assert n_heads % hp == 0 and all(off % hp == 0 for off in col_blocks)
    rel = jnp.pad(rel_bias, ((0, 0), (0, rp - N_REL))).reshape(n_heads, 1, rp)

    def head_spec(off):
        return pl.BlockSpec((s_len, hp * HEAD_DIM), lambda h, b: (b, off // hp + h))

    return pl.pallas_call(
        functools.partial(_chunk_kernel, tq=tq, scale=scale, hp=hp),
        grid=(n_heads // hp, bsz),
        in_specs=[head_spec(qb), head_spec(kb), head_spec(vb), head_spec(gb),
                  pl.BlockSpec((hp, 1, rp), lambda h, b: (h, 0, 0))],
        out_specs=pl.BlockSpec((s_len, hp * HEAD_DIM), lambda h, b: (b, h)),
        out_shape=jax.ShapeDtypeStruct((bsz * s_len, n_heads * HEAD_DIM), BF16),
        scratch_shapes=[pltpu.VMEM((s_len + pad, hp * HEAD_DIM), BF16),
                        pltpu.VMEM((s_len + pad, 2 * hp * HEAD_DIM), BF16),
                        pltpu.VMEM((hp, tq, tq + pad), F32)],
        compiler_params=_params(2),
        name="chunk_attention",
    )(proj, proj, proj, proj, rel)


def _sb_kernel(q_ref, k_ref, v_ref, g_ref, o_ref, *, ts, tl, rows_per_iter, scale2, hp):
    s_len = q_ref.shape[0]
    heads = [slice(i * HEAD_DIM, (i + 1) * HEAD_DIM) for i in range(hp)]

    def after2(n):
        a = lax.broadcasted_iota(jnp.int32, (n, n), 0) > lax.broadcasted_iota(jnp.int32, (n, n), 1)
        a = a.astype(F32).astype(BF16)
        return jnp.concatenate([a, a], axis=0)

    tri = {n: after2(n) for n in {ts, tl}}
    strict = lax.broadcasted_iota(jnp.int32, (ts, ts), 1) < lax.broadcasted_iota(jnp.int32, (ts, ts), 0)

    def log2_terms(q, k):
        z = _dot_nt(q, k) * scale2
        lb = jnp.minimum(z, 0.0) - jnp.log2(1.0 + jnp.exp2(-jnp.abs(z)))
        return lb, lb - z

    def suffix_sums(lom):
        hi = lom.astype(BF16)
        lo = (lom - hi.astype(F32)).astype(BF16)
        tail = _dot(jnp.concatenate([hi, lo], axis=1), tri[lom.shape[1]])
        return tail, tail[:, 0:1] + lom[:, 0:1]

    def first_visit(chains):
        rows = [pl.ds(r0, ts) for _, r0, _ in chains]
        keys = [pl.ds(r0 - lw, lw + ts) for _, r0, lw in chains]
        qs = [q_ref[rs, hs] for rs, (hs, _, _) in zip(rows, chains)]
        terms = [log2_terms(q, k_ref[ks, hs]) for q, ks, (hs, _, _) in zip(qs, keys, chains)]
        sums = []
        for (lb, lom), (_, _, lw) in zip(terms, chains):
            tail_d, rsum = suffix_sums(jnp.where(strict, lom[:, lw:], 0.0))
            tail_l = None
            if lw:
                tail_l, rsum_l = suffix_sums(lom[:, :lw])
                tail_l, rsum = tail_l + rsum, rsum + rsum_l
            sums.append((tail_l, tail_d, rsum))
        accs = []
        for (lb, _), (tail_l, tail_d, _), ks, (hs, _, lw) in zip(terms, sums, keys, chains):
            a = jnp.where(strict, jnp.exp2(lb[:, lw:] + tail_d), 0.0)
            if lw:
                a = jnp.concatenate([jnp.exp2(lb[:, :lw] + tail_l), a], axis=1)
            accs.append(_dot(a.astype(BF16), v_ref[ks, hs]))
        gates = [_silu(g_ref[rs, hs].astype(F32)) for rs, (hs, _, _) in zip(rows, chains)]
        for rs, (hs, _, _), acc, gate in zip(rows, chains, accs, gates):
            o_ref[rs, hs] = (acc * gate).astype(o_ref.dtype)
        return [(q, s[2], acc, gate) for q, s, acc, gate in zip(qs, sums, accs, gates)]

    def more(carry):
        k0, rsum, _ = carry
        return jnp.logical_and(k0 >= 0, jnp.max(rsum) >= SB_STOP_LOG2)

    def visit(chains):
        carries = first_visit(chains)

        worst = functools.reduce(jnp.maximum, [c[1] for c in carries])

        @pl.when(jnp.max(worst) >= SB_STOP_LOG2)
        def _visit_more_keys():
            for (hs, r0, lw), (q, rsum, acc, gate) in zip(chains, carries):
                def tile(carry, q=q, hs=hs):
                    k0, rsum, acc = carry
                    ks = pl.ds(pl.multiple_of(k0, ts), ts)
                    lb, lom = log2_terms(q, k_ref[ks, hs])
                    tail, rs = suffix_sums(lom)
                    a = jnp.exp2(lb + tail + rsum)
                    return k0 - ts, rsum + rs, acc + _dot(a.astype(BF16), v_ref[ks, hs])

                k_next = jnp.asarray(r0 - lw - ts, jnp.int32)
                acc = lax.while_loop(more, tile, (k_next, rsum, acc))[2]
                o_ref[pl.ds(r0, ts), hs] = (acc * gate).astype(o_ref.dtype)

    visit([(hs, r0, min(r0, tl)) for r0 in range(0, rows_per_iter, ts) for hs in heads])

    def q_body(it, _):
        q0 = pl.multiple_of((it + 1) * rows_per_iter, rows_per_iter)
        visit([(hs, q0 + off, tl) for off in range(0, rows_per_iter, ts) for hs in heads])
        return 0

    lax.fori_loop(0, s_len // rows_per_iter - 1, q_body, 0)


def _sb_attention(proj, bsz, s_len, n_heads, col_blocks, ts=128, tl=256, rows_per_iter=512, hp=4):
    qb, kb, vb, gb = col_blocks
    scale2 = LOG2E * HEAD_DIM ** -0.5
    assert n_heads % hp == 0 and all(off % hp == 0 for off in col_blocks)
    assert tl % ts == 0 and rows_per_iter % ts == 0 and s_len % rows_per_iter == 0

    def head_spec(off):
        return pl.BlockSpec((s_len, hp * HEAD_DIM), lambda b, h: (b, off // hp + h))

    return pl.pallas_call(
        functools.partial(_sb_kernel, ts=ts, tl=tl, rows_per_iter=rows_per_iter, scale2=scale2, hp=hp),
        grid=(bsz, n_heads // hp),
        in_specs=[head_spec(qb), head_spec(kb), head_spec(vb), head_spec(gb)],
        out_specs=pl.BlockSpec((s_len, hp * HEAD_DIM), lambda b, h: (b, h)),
        out_shape=jax.ShapeDtypeStruct((bsz * s_len, n_heads * HEAD_DIM), BF16),
        compiler_params=_params(2),
        name="sb_attention",
    )(proj, proj, proj, proj)


def _even_layer(x2, h, bsz, s_len, g_post, g_next, w_in, b_f, rel_bias, w_out):
    n_a = b_f.shape[0]
    n_b = rel_bias.shape[0]
    main_cols = 4 * (n_a + n_b) * HEAD_DIM
    w_in_t = w_in.T
    proj, w_out16 = _in_proj(h, w_in_t, w_out, BF16, bm=PROJ_BM, bn=PROJ_BN,
                             n_col_blocks=main_cols // PROJ_BN, w_transposed=True, name="in_proj_even")
    af, aft = _forget_proj(h, w_in_t, main_cols, n_a)
    c_col, c_row = _forget_cumsum(af, aft, b_f, bsz)
    mix_a = _fox_attention(proj, c_row, c_col, bsz, s_len, n_a,
                           (0, n_a, 2 * n_a, 3 * n_a))
    off = 4 * n_a
    mix_b = _chunk_attention(proj, rel_bias, bsz, s_len, n_b,
                             (off, off + n_b, off + 2 * n_b, off + 3 * n_b))
    return _out_proj_norm([mix_a, mix_b], w_out16, x2, g_post, g_next,
                          bm=OUT_BM, bn=OUT_BN, name="out_proj_norm_even")


def _odd_layer(x2, h, bsz, s_len, g_post, g_next, w_in, w_out):
    n_c = w_out.shape[0] // HEAD_DIM
    proj, w_out16 = _in_proj(h, w_in, w_out, BF16, bm=PROJ_BM, bn=PROJ_BN,
                             n_col_blocks=w_in.shape[1] // PROJ_BN, name="in_proj_odd")
    mix = _sb_attention(proj, bsz, s_len, n_c, (0, n_c, 2 * n_c, 3 * n_c))
    return _out_proj_norm([mix], w_out16, x2, g_post, g_next,
                          bm=OUT_BM, bn=OUT_BN, name="out_proj_norm_odd")


def kernel(x, norm_pre, norm_post, w_in_even, b_f_even, rel_bias_even, w_out_even, w_in_odd, w_out_odd):
    bsz, s_len, d = x.shape
    depth = norm_pre.shape[0]
    x2 = x.reshape(bsz * s_len, d)
    h = _rmsnorm(x2, norm_pre[0], BF16)
    for layer in range(depth):
        i = layer // 2
        g_next = norm_pre[layer + 1] if layer + 1 < depth else None
        if layer % 2 == 0:
            x2, h = _even_layer(x2, h, bsz, s_len, norm_post[layer], g_next,
                                w_in_even[i], b_f_even[i], rel_bias_even[i], w_out_even[i])
        else:
            x2, h = _odd_layer(x2, h, bsz, s_len, norm_post[layer], g_next,
                               w_in_odd[i], w_out_odd[i])
    return x2.reshape(bsz, s_len, d)
```

```python
import functools

import jax
import jax.numpy as jnp
from jax import lax
from jax.experimental import pallas as pl
from jax.experimental.pallas import tpu as pltpu

HEAD_DIM = 128
CHUNK = 64
LEFT_CHUNKS = 8
REL_CLIP = 128
N_REL = 2 * REL_CLIP + 1
RMS_EPS = 1e-6

LANES = 128
VMEM_LIMIT = 60 * 1024 * 1024
PROJ_BM, PROJ_BN = 512, 1024
OUT_BM, OUT_BN = 512, 512
NORM_ROWS = 128
NEG = -1e30
LOG2E = 1.4426950408889634
SB_STOP_LOG2 = -152.0

BF16 = jnp.bfloat16
F32 = jnp.float32


def _params(n_axes):
    return pltpu.CompilerParams(dimension_semantics=("arbitrary",) * n_axes,
                                vmem_limit_bytes=VMEM_LIMIT)


def _dot_nt(a, b):
    return lax.dot_general(a, b, (((1,), (1,)), ((), ())), preferred_element_type=F32)


def _dot(a, b):
    return jnp.dot(a, b, preferred_element_type=F32)


def _split3(x):
    hi = x.astype(BF16)
    r1 = x - hi.astype(F32)
    mid = r1.astype(BF16)
    lo = (r1 - mid.astype(F32)).astype(BF16)
    return hi, mid, lo


def _log_sigmoid(z):
    return jnp.minimum(z, 0.0) - jnp.log1p(jnp.exp(-jnp.abs(z)))


def _silu(g):
    return g / (1.0 + jnp.exp(-g))


def _rmsnorm_kernel(x_ref, g_ref, o_ref):
    x = x_ref[...]
    ms = jnp.mean(x * x, axis=-1, keepdims=True)
    o_ref[...] = (x * lax.rsqrt(ms + RMS_EPS) * g_ref[...]).astype(o_ref.dtype)


def _rmsnorm(x, g, out_dtype, bm=256):
    m, d = x.shape
    return pl.pallas_call(
        _rmsnorm_kernel,
        grid=(m // bm,),
        in_specs=[pl.BlockSpec((bm, d), lambda i: (i, 0)),
                  pl.BlockSpec((1, d), lambda i: (0, 0))],
        out_specs=pl.BlockSpec((bm, d), lambda i: (i, 0)),
        out_shape=jax.ShapeDtypeStruct((m, d), out_dtype),
        compiler_params=_params(1),
        name="rmsnorm",
    )(x, g.reshape(1, d))


def _out_proj_norm_kernel(*refs, n_parts, bn, with_next):
    mix_refs, w_refs = refs[:n_parts], refs[n_parts:2 * n_parts]
    rest = list(refs[2 * n_parts:])
    xres_ref, gpost_ref = rest.pop(0), rest.pop(0)
    gnext_ref = rest.pop(0) if with_next else None
    o_ref = rest.pop(0)
    h_ref = rest.pop(0) if with_next else None
    xs_ref = rest.pop(0)
    j = pl.program_id(1)
    c0 = pl.multiple_of(j * bn, bn)

    y = _dot(mix_refs[0][...], w_refs[0][...])
    for mix_ref, w_ref in zip(mix_refs[1:], w_refs[1:]):
        y = y + _dot(mix_ref[...], w_ref[...])
    o_ref[:, pl.ds(c0, bn)] = y
    xs_ref[:, pl.ds(c0, bn)] = xres_ref[...]

    @pl.when(j == pl.num_programs(1) - 1)
    def _normalise():
        def rows(r, _):
            rs = pl.ds(pl.multiple_of(r * NORM_ROWS, NORM_ROWS), NORM_ROWS)
            y = o_ref[rs, :]
            ms = jnp.mean(y * y, axis=-1, keepdims=True)
            x_new = xs_ref[rs, :] + y * lax.rsqrt(ms + RMS_EPS) * gpost_ref[...]
            o_ref[rs, :] = x_new
            if with_next:
                ms = jnp.mean(x_new * x_new, axis=-1, keepdims=True)
                h_ref[rs, :] = (x_new * lax.rsqrt(ms + RMS_EPS) * gnext_ref[...]).astype(h_ref.dtype)
            return 0

        lax.fori_loop(0, o_ref.shape[0] // NORM_ROWS, rows, 0)


def _out_proj_norm(mixes, w16, x, g_post, g_next, *, bm, bn, name):
    m, d = x.shape
    kp = mixes[0].shape[1]
    n_parts = len(mixes)
    with_next = g_next is not None
    assert all(t.shape == (m, kp) for t in mixes) and w16.shape == (n_parts * kp, d)
    row_block = pl.BlockSpec((bm, d), lambda i, j: (i, 0))
    gain = pl.BlockSpec((1, d), lambda i, j: (0, 0))
    in_specs = ([pl.BlockSpec((bm, kp), lambda i, j: (i, 0)) for _ in mixes]
                + [pl.BlockSpec((kp, bn), lambda i, j, p=p: (p, j)) for p in range(n_parts)]
                + [pl.BlockSpec((bm, bn), lambda i, j: (i, j)), gain] + ([gain] if with_next else []))
    operands = list(mixes) + [w16] * n_parts + [x, g_post.reshape(1, d)]
    out_specs, out_shape = [row_block], [jax.ShapeDtypeStruct((m, d), F32)]
    if with_next:
        operands.append(g_next.reshape(1, d))
        out_specs.append(row_block)
        out_shape.append(jax.ShapeDtypeStruct((m, d), BF16))
    outs = pl.pallas_call(
        functools.partial(_out_proj_norm_kernel, n_parts=n_parts, bn=bn, with_next=with_next),
        grid=(m // bm, d // bn),
        in_specs=in_specs,
        out_specs=out_specs,
        out_shape=out_shape,
        scratch_shapes=[pltpu.VMEM((bm, d), F32)],
        compiler_params=_params(2),
        name=name,
    )(*operands)
    return (outs[0], outs[1]) if with_next else (outs[0], None)


def _in_proj_kernel(x_ref, w_ref, side_ref, o_ref, side16_ref, w16_ref, *, w_transposed):
    @pl.when(pl.program_id(1) == 0)
    def _round_weights():
        w16_ref[...] = w_ref[...].astype(BF16)

    side16_ref[...] = side_ref[...].astype(BF16)
    dot = _dot_nt if w_transposed else _dot
    o_ref[...] = dot(x_ref[...], w16_ref[...]).astype(o_ref.dtype)


def _in_proj(x, w, side, out_dtype, *, bm, bn, n_col_blocks, w_transposed=False, name="in_proj"):
    m, k = x.shape
    n_row_blocks = m // bm
    n_steps = n_col_blocks * n_row_blocks
    side_rows = side.shape[0] // n_steps
    assert w.shape[1 if w_transposed else 0] == k
    assert side_rows * n_steps == side.shape[0] and side_rows % 16 == 0
    w_block = (bn, k) if w_transposed else (k, bn)
    side_spec = pl.BlockSpec((side_rows, side.shape[1]), lambda j, i: (j * n_row_blocks + i, 0))
    return pl.pallas_call(
        functools.partial(_in_proj_kernel, w_transposed=w_transposed),
        grid=(n_col_blocks, n_row_blocks),
        in_specs=[pl.BlockSpec((bm, k), lambda j, i: (i, 0)),
                  pl.BlockSpec(w_block, (lambda j, i: (j, 0)) if w_transposed else (lambda j, i: (0, j))),
                  side_spec],
        out_specs=[pl.BlockSpec((bm, bn), lambda j, i: (i, j)), side_spec],
        out_shape=[jax.ShapeDtypeStruct((m, n_col_blocks * bn), out_dtype),
                   jax.ShapeDtypeStruct(side.shape, BF16)],
        scratch_shapes=[pltpu.VMEM(w_block, BF16)],
        compiler_params=_params(2),
        name=name,
    )(x, w, side)


def _forget_proj_kernel(x_ref, wt_ref, af_ref):
    af_ref[...] = _dot_nt(x_ref[...], wt_ref[...].astype(BF16))


def _forget_proj(x, w_t, row0, n_rows, bm=1024):
    m, k = x.shape
    assert row0 % n_rows == 0 and w_t.shape[1] == k
    return pl.pallas_call(
        _forget_proj_kernel,
        grid=(m // bm,),
        in_specs=[pl.BlockSpec((bm, k), lambda i: (i, 0)),
                  pl.BlockSpec((n_rows, k), lambda i: (row0 // n_rows, 0))],
        out_specs=pl.BlockSpec((bm, n_rows), lambda i: (i, 0)),
        out_shape=jax.ShapeDtypeStruct((m, n_rows), F32),
        compiler_params=_params(1),
        name="forget_proj",
    )(x, w_t)


def _forget_cumsum_kernel(af_ref, brow_ref, ccol_ref, crow_ref, *, tc):
    s_len, nh = af_ref.shape
    r = lax.broadcasted_iota(jnp.int32, (tc, tc), 0)
    c = lax.broadcasted_iota(jnp.int32, (tc, tc), 1)
    lower = (c <= r).astype(F32).astype(BF16)
    upper = (r <= c).astype(F32).astype(BF16)
    eye = (lax.broadcasted_iota(jnp.int32, (nh, nh), 0)
           == lax.broadcasted_iota(jnp.int32, (nh, nh), 1)).astype(F32).astype(BF16)

    def body(i, carry):
        carry_row, carry_col = carry
        t0 = pl.multiple_of(i * tc, tc)
        lf = _log_sigmoid(af_ref[pl.ds(t0, tc), :] + brow_ref[...])
        a, b, d = _split3(lf)
        cc = (_dot(lower, a) + _dot(lower, b) + _dot(lower, d)) + carry_row
        a, b, d = (_dot_nt(eye, t).astype(BF16) for t in (a, b, d))
        cr = (_dot(a, upper) + _dot(b, upper) + _dot(d, upper)) + carry_col
        ccol_ref[pl.ds(t0, tc), :] = cc * LOG2E
        crow_ref[:, pl.ds(t0, tc)] = cr * LOG2E
        return cc[tc - 1:tc, :], cr[:, tc - 1:tc]

    lax.fori_loop(0, s_len // tc, body,
                  (jnp.zeros((1, nh), F32), jnp.zeros((nh, 1), F32)))


def _forget_cumsum(af, b_f, bsz, tc=256):
    m, nh = af.shape
    s_len = m // bsz
    return pl.pallas_call(
        functools.partial(_forget_cumsum_kernel, tc=tc),
        grid=(bsz,),
        in_specs=[pl.BlockSpec((s_len, nh), lambda b: (b, 0)),
                  pl.BlockSpec((1, nh), lambda b: (0, 0))],
        out_specs=[pl.BlockSpec((s_len, nh), lambda b: (b, 0)),
                   pl.BlockSpec((nh, s_len), lambda b: (0, b))],
        out_shape=[jax.ShapeDtypeStruct((m, nh), F32),
                   jax.ShapeDtypeStruct((nh, m), F32)],
        compiler_params=_params(1),
        name="forget_cumsum",
    )(af, b_f.reshape(1, nh))


def _fox_kernel(q_ref, k_ref, v_ref, g_ref, crow_ref, ccol_ref, o_ref, vone_ref, *, tq, tk, scale2, hp):
    h0 = pl.program_id(1) * hp
    s_len = q_ref.shape[0]
    lane = lax.broadcasted_iota(jnp.int32, (tq, ccol_ref.shape[-1]), 1)
    heads = [slice(i * HEAD_DIM, (i + 1) * HEAD_DIM) for i in range(hp)]
    wide = [slice(2 * i * HEAD_DIM, 2 * (i + 1) * HEAD_DIM) for i in range(hp)]
    for hs, ws in zip(heads, wide):
        vone_ref[:, ws] = jnp.concatenate(
            [v_ref[:, hs], jnp.ones((s_len, HEAD_DIM), vone_ref.dtype)], axis=1)

    def q_body(qi, _):
        q0 = pl.multiple_of(qi * tq, tq)
        qs = [q_ref[pl.ds(q0, tq), hs] for hs in heads]
        c_blk = ccol_ref[pl.ds(q0, tq), :]
        c_ts = [jnp.sum(jnp.where(lane == h0 + i, c_blk, 0.0), axis=1, keepdims=True) for i in range(hp)]
        n_full = q0 // tk

        def logits(k0, width):
            return tuple(_dot_nt(qs[i], k_ref[pl.ds(k0, width), hs]) * scale2 - crow_ref[i, :, pl.ds(k0, width)]
                         for i, hs in enumerate(heads))

        def absorb(us, k0, width, state, diag_offset=None):
            if diag_offset is not None:
                keep = (lax.broadcasted_iota(jnp.int32, (tq, width), 1)
                        <= lax.broadcasted_iota(jnp.int32, (tq, width), 0) + diag_offset)
                us = [jnp.where(keep, u, NEG) for u in us]
            m_news = [jnp.maximum(st[0], jnp.max(u, axis=1, keepdims=True) + c_t)
                      for u, st, c_t in zip(us, state, c_ts)]
            ps = [jnp.exp2(u - (m_new - c_t)).astype(BF16) for u, m_new, c_t in zip(us, m_news, c_ts)]
            pvs = [_dot(p, vone_ref[pl.ds(k0, width), ws]) for p, ws in zip(ps, wide)]
            return tuple((m_new, jnp.exp2(m - m_new) * acc + pv)
                         for (m, acc), m_new, pv in zip(state, m_news, pvs))

        def absorb_tiles(tiles, state):
            all_us = [logits(k0, width) for k0, width, _ in tiles]
            for us, (k0, width, diag_offset) in zip(all_us, tiles):
                state = absorb(us, k0, width, state, diag_offset)
            return state

        def step2(kp, state):
            k0 = pl.multiple_of(2 * kp * tk, tk)
            return absorb_tiles([(k0, tk, None), (k0 + tk, tk, None)], state)

        state = tuple((jnp.full((tq, 1), NEG, F32), jnp.zeros((tq, 2 * HEAD_DIM), F32)) for _ in heads)
        state = lax.fori_loop(0, n_full // 2, step2, state)
        k_diag = pl.multiple_of(n_full * tk, tk)
        n_r = tk // tq

        def last_branch(odd, r):
            width = (r + 1) * tq

            def run():
                gates = [_silu(g_ref[pl.ds(q0, tq), hs].astype(F32)) for hs in heads]
                tiles = [(pl.multiple_of(k_diag - tk, tk), tk, None)] if odd else []
                final = absorb_tiles(tiles + [(k_diag, width, r * tq)], state)
                for (_, acc), gate, hs in zip(final, gates, heads):
                    o_ref[pl.ds(q0, tq), hs] = ((acc[:, :HEAD_DIM] / acc[:, HEAD_DIM:]) * gate).astype(o_ref.dtype)
                return 0

            return run

        return lax.switch((n_full % 2) * n_r + qi % n_r,
                          [last_branch(odd, r) for odd in (0, 1) for r in range(n_r)])

    lax.fori_loop(0, s_len // tq, q_body, 0)


def _fox_attention(proj, c_row, c_col, bsz, s_len, n_heads, col_blocks, tq=256, tk=1024, hp=2):
    qb, kb, vb, gb = col_blocks
    scale2 = LOG2E * HEAD_DIM ** -0.5
    assert n_heads % hp == 0 and all(off % hp == 0 for off in col_blocks)
    assert tk % tq == 0 and s_len % tk == 0

    def head_spec(off):
        return pl.BlockSpec((s_len, hp * HEAD_DIM), lambda b, h: (b, off // hp + h))

    return pl.pallas_call(
        functools.partial(_fox_kernel, tq=tq, tk=tk, scale2=scale2, hp=hp),
        grid=(bsz, n_heads // hp),
        in_specs=[head_spec(qb), head_spec(kb), head_spec(vb), head_spec(gb),
                  pl.BlockSpec((hp, 1, s_len), lambda b, h: (h, 0, b)),
                  pl.BlockSpec((s_len, n_heads), lambda b, h: (b, 0))],
        out_specs=pl.BlockSpec((s_len, hp * HEAD_DIM), lambda b, h: (b, h)),
        out_shape=jax.ShapeDtypeStruct((bsz * s_len, n_heads * HEAD_DIM), BF16),
        scratch_shapes=[pltpu.VMEM((s_len, hp * 2 * HEAD_DIM), BF16)],
        compiler_params=_params(2),
        name="fox_attention",
    )(proj, proj, proj, proj, c_row.reshape(n_heads, 1, bsz * s_len), c_col)


def _chunk_kernel(q_ref, k_ref, v_ref, g_ref, rel_ref, o_ref, kpad, vpad, bias_ref, *, tq, scale, hp):
    b = pl.program_id(1)
    s_len = q_ref.shape[0]
    pad = LEFT_CHUNKS * CHUNK
    win = tq + pad
    rp = rel_ref.shape[-1]
    heads = [slice(i * HEAD_DIM, (i + 1) * HEAD_DIM) for i in range(hp)]

    @pl.when(b == 0)
    def _build_bias():
        u = lax.broadcasted_iota(jnp.int32, (rp, win), 1)
        r = lax.broadcasted_iota(jnp.int32, (rp, win), 0)
        idx = jnp.where(u < pad + CHUNK, jnp.clip(pad - u, -REL_CLIP, REL_CLIP) + REL_CLIP, 2 * REL_CLIP)
        sel = (r == idx).astype(F32).astype(BF16)
        qc = lax.broadcasted_iota(jnp.int32, (tq, win), 0) // CHUNK
        kc = lax.broadcasted_iota(jnp.int32, (tq, win), 1) // CHUNK
        in_band = (kc >= qc) & (kc <= qc + LEFT_CHUNKS)
        for i in range(hp):
            a, bb, d = _split3(jnp.broadcast_to(rel_ref[i], (8, rp)))
            g = _dot(a, sel) + _dot(bb, sel) + _dot(d, sel)
            tile = pltpu.roll(jnp.broadcast_to(g[0:1, :], (tq, win)), 0, 1, stride=1, stride_axis=0)
            bias_ref[i] = jnp.where(in_band, tile, NEG)

    kpad[pl.ds(0, pad), :] = jnp.zeros((pad, hp * HEAD_DIM), kpad.dtype)
    kpad[pl.ds(pad, s_len), :] = k_ref[...]
    wide = [slice(2 * i * HEAD_DIM, 2 * (i + 1) * HEAD_DIM) for i in range(hp)]
    vpad[pl.ds(0, pad), :] = jnp.zeros((pad, 2 * hp * HEAD_DIM), vpad.dtype)
    for hs, ws in zip(heads, wide):
        vpad[pl.ds(pad, s_len), ws] = jnp.concatenate(
            [v_ref[:, hs], jnp.ones((s_len, HEAD_DIM), vpad.dtype)], axis=1)
    key_pos = lax.broadcasted_iota(jnp.int32, (tq, win), 1)

    def body(j, _, before_start):
        q0 = pl.multiple_of(j * tq, tq)
        ss = [_dot_nt(q_ref[pl.ds(q0, tq), hs], kpad[pl.ds(q0, win), hs]) * scale + bias_ref[i]
              for i, hs in enumerate(heads)]
        if before_start:
            ss = [jnp.where(key_pos + q0 >= pad, s, NEG) for s in ss]
        ps = [jnp.exp(s - jnp.max(s, axis=1, keepdims=True)).astype(BF16) for s in ss]
        pvs = [_dot(p, vpad[pl.ds(q0, win), ws]) for p, ws in zip(ps, wide)]
        for pv, hs in zip(pvs, heads):
            o = pv[:, :HEAD_DIM] / pv[:, HEAD_DIM:]
            g = g_ref[pl.ds(q0, tq), hs].astype(F32)
            o_ref[pl.ds(q0, tq), hs] = (o * _silu(g)).astype(o_ref.dtype)
        return 0

    n_start = pad // tq
    lax.fori_loop(0, n_start, functools.partial(body, before_start=True), 0)
    lax.fori_loop(n_start, s_len // tq, functools.partial(body, before_start=False), 0)


def _chunk_attention(proj, rel_bias, bsz, s_len, n_heads, col_blocks, tq=256, hp=4):
    qb, kb, vb, gb = col_blocks
    scale = HEAD_DIM ** -0.5
    pad = LEFT_CHUNKS * CHUNK
    rp = 3 * LANES
    assert N_REL <= rp and pad % tq == 0
    assert n_heads % hp == 0 and all(off % hp == 0 for off in col_blocks)
    rel = jnp.pad(rel_bias, ((0, 0), (0, rp - N_REL))).reshape(n_heads, 1, rp)

    def head_spec(off):
        return pl.BlockSpec((s_len, hp * HEAD_DIM), lambda h, b: (b, off // hp + h))

    return pl.pallas_call(
        functools.partial(_chunk_kernel, tq=tq, scale=scale, hp=hp),
        grid=(n_heads // hp, bsz),
        in_specs=[head_spec(qb), head_spec(kb), head_spec(vb), head_spec(gb),
                  pl.BlockSpec((hp, 1, rp), lambda h, b: (h, 0, 0))],
        out_specs=pl.BlockSpec((s_len, hp * HEAD_DIM), lambda h, b: (b, h)),
        out_shape=jax.ShapeDtypeStruct((bsz * s_len, n_heads * HEAD_DIM), BF16),
        scratch_shapes=[pltpu.VMEM((s_len + pad, hp * HEAD_DIM), BF16),
                        pltpu.VMEM((s_len + pad, 2 * hp * HEAD_DIM), BF16),
                        pltpu.VMEM((hp, tq, tq + pad), F32)],
        compiler_params=_params(2),
        name="chunk_attention",
    )(proj, proj, proj, proj, rel)


def _sb_kernel(q_ref, k_ref, v_ref, g_ref, o_ref, *, ts, tl, rows_per_iter, scale2, hp):
    s_len = q_ref.shape[0]
    heads = [slice(i * HEAD_DIM, (i + 1) * HEAD_DIM) for i in range(hp)]

    def after2(n):
        a = lax.broadcasted_iota(jnp.int32, (n, n), 0) > lax.broadcasted_iota(jnp.int32, (n, n), 1)
        a = a.astype(F32).astype(BF16)
        return jnp.concatenate([a, a], axis=0)

    tri = {n: after2(n) for n in {ts, tl}}
    strict = lax.broadcasted_iota(jnp.int32, (ts, ts), 1) < lax.broadcasted_iota(jnp.int32, (ts, ts), 0)

    def log2_terms(q, k):
        z = _dot_nt(q, k) * scale2
        lb = jnp.minimum(z, 0.0) - jnp.log2(1.0 + jnp.exp2(-jnp.abs(z)))
        return lb, lb - z

    def suffix_sums(lom):
        hi = lom.astype(BF16)
        lo = (lom - hi.astype(F32)).astype(BF16)
        tail = _dot(jnp.concatenate([hi, lo], axis=1), tri[lom.shape[1]])
        return tail, tail[:, 0:1] + lom[:, 0:1]

    def first_visit(chains):
        rows = [pl.ds(r0, ts) for _, r0, _ in chains]
        keys = [pl.ds(r0 - lw, lw + ts) for _, r0, lw in chains]
        qs = [q_ref[rs, hs] for rs, (hs, _, _) in zip(rows, chains)]
        terms = [log2_terms(q, k_ref[ks, hs]) for q, ks, (hs, _, _) in zip(qs, keys, chains)]
        sums = []
        for (lb, lom), (_, _, lw) in zip(terms, chains):
            tail_d, rsum = suffix_sums(jnp.where(strict, lom[:, lw:], 0.0))
            tail_l = None
            if lw:
                tail_l, rsum_l = suffix_sums(lom[:, :lw])
                tail_l, rsum = tail_l + rsum, rsum + rsum_l
            sums.append((tail_l, tail_d, rsum))
        accs = []
        for (lb, _), (tail_l, tail_d, _), ks, (hs, _, lw) in zip(terms, sums, keys, chains):
            a = jnp.where(strict, jnp.exp2(lb[:, lw:] + tail_d), 0.0)
            if lw:
                a = jnp.concatenate([jnp.exp2(lb[:, :lw] + tail_l), a], axis=1)
            accs.append(_dot(a.astype(BF16), v_ref[ks, hs]))
        gates = [_silu(g_ref[rs, hs].astype(F32)) for rs, (hs, _, _) in zip(rows, chains)]
        for rs, (hs, _, _), acc, gate in zip(rows, chains, accs, gates):
            o_ref[rs, hs] = (acc * gate).astype(o_ref.dtype)
        return [(q, s[2], acc, gate) for q, s, acc, gate in zip(qs, sums, accs, gates)]

    def more(carry):
        k0, rsum, _ = carry
        return jnp.logical_and(k0 >= 0, jnp.max(rsum) >= SB_STOP_LOG2)

    def visit(chains):
        carries = first_visit(chains)

        worst = functools.reduce(jnp.maximum, [c[1] for c in carries])

        @pl.when(jnp.max(worst) >= SB_STOP_LOG2)
        def _visit_more_keys():
            for (hs, r0, lw), (q, rsum, acc, gate) in zip(chains, carries):
                def tile(carry, q=q, hs=hs):
                    k0, rsum, acc = carry
                    ks = pl.ds(pl.multiple_of(k0, ts), ts)
                    lb, lom = log2_terms(q, k_ref[ks, hs])
                    tail, rs = suffix_sums(lom)
                    a = jnp.exp2(lb + tail + rsum)
                    return k0 - ts, rsum + rs, acc + _dot(a.astype(BF16), v_ref[ks, hs])

                k_next = jnp.asarray(r0 - lw - ts, jnp.int32)
                acc = lax.while_loop(more, tile, (k_next, rsum, acc))[2]
                o_ref[pl.ds(r0, ts), hs] = (acc * gate).astype(o_ref.dtype)

    visit([(hs, r0, min(r0, tl)) for r0 in range(0, rows_per_iter, ts) for hs in heads])

    def q_body(it, _):
        q0 = pl.multiple_of((it + 1) * rows_per_iter, rows_per_iter)
        visit([(hs, q0 + off, tl) for off in range(0, rows_per_iter, ts) for hs in heads])
        return 0

    lax.fori_loop(0, s_len // rows_per_iter - 1, q_body, 0)


def _sb_attention(proj, bsz, s_len, n_heads, col_blocks, ts=128, tl=256, rows_per_iter=512, hp=4):
    qb, kb, vb, gb = col_blocks
    scale2 = LOG2E * HEAD_DIM ** -0.5
    assert n_heads % hp == 0 and all(off % hp == 0 for off in col_blocks)
    assert tl % ts == 0 and rows_per_iter % ts == 0 and s_len % rows_per_iter == 0

    def head_spec(off):
        return pl.BlockSpec((s_len, hp * HEAD_DIM), lambda b, h: (b, off // hp + h))

    return pl.pallas_call(
        functools.partial(_sb_kernel, ts=ts, tl=tl, rows_per_iter=rows_per_iter, scale2=scale2, hp=hp),
        grid=(bsz, n_heads // hp),
        in_specs=[head_spec(qb), head_spec(kb), head_spec(vb), head_spec(gb)],
        out_specs=pl.BlockSpec((s_len, hp * HEAD_DIM), lambda b, h: (b, h)),
        out_shape=jax.ShapeDtypeStruct((bsz * s_len, n_heads * HEAD_DIM), BF16),
        compiler_params=_params(2),
        name="sb_attention",
    )(proj, proj, proj, proj)


def _even_layer(x2, h, bsz, s_len, g_post, g_next, w_in, b_f, rel_bias, w_out):
    n_a = b_f.shape[0]
    n_b = rel_bias.shape[0]
    main_cols = 4 * (n_a + n_b) * HEAD_DIM
    w_in_t = w_in.T
    proj, w_out16 = _in_proj(h, w_in_t, w_out, BF16, bm=PROJ_BM, bn=PROJ_BN,
                             n_col_blocks=main_cols // PROJ_BN, w_transposed=True, name="in_proj_even")
    af = _forget_proj(h, w_in_t, main_cols, n_a)
    c_col, c_row = _forget_cumsum(af, b_f, bsz)
    mix_a = _fox_attention(proj, c_row, c_col, bsz, s_len, n_a,
                           (0, n_a, 2 * n_a, 3 * n_a))
    off = 4 * n_a
    mix_b = _chunk_attention(proj, rel_bias, bsz, s_len, n_b,
                             (off, off + n_b, off + 2 * n_b, off + 3 * n_b))
    return _out_proj_norm([mix_a, mix_b], w_out16, x2, g_post, g_next,
                          bm=OUT_BM, bn=OUT_BN, name="out_proj_norm_even")


def _odd_layer(x2, h, bsz, s_len, g_post, g_next, w_in, w_out):
    n_c = w_out.shape[0] // HEAD_DIM
    proj, w_out16 = _in_proj(h, w_in, w_out, BF16, bm=PROJ_BM, bn=PROJ_BN,
                             n_col_blocks=w_in.shape[1] // PROJ_BN, name="in_proj_odd")
    mix = _sb_attention(proj, bsz, s_len, n_c, (0, n_c, 2 * n_c, 3 * n_c))
    return _out_proj_norm([mix], w_out16, x2, g_post, g_next,
                          bm=OUT_BM, bn=OUT_BN, name="out_proj_norm_odd")


def kernel(x, norm_pre, norm_post, w_in_even, b_f_even, rel_bias_even, w_out_even, w_in_odd, w_out_odd):
    bsz, s_len, d = x.shape
    depth = norm_pre.shape[0]
    x2 = x.reshape(bsz * s_len, d)
    h = _rmsnorm(x2, norm_pre[0], BF16)
    for layer in range(depth):
        i = layer // 2
        g_next = norm_pre[layer + 1] if layer + 1 < depth else None
        if layer % 2 == 0:
            x2, h = _even_layer(x2, h, bsz, s_len, norm_post[layer], g_next,
                                w_in_even[i], b_f_even[i], rel_bias_even[i], w_out_even[i])
        else:
            x2, h = _odd_layer(x2, h, bsz, s_len, norm_post[layer], g_next,
                               w_in_odd[i], w_out_odd[i])
    return x2.reshape(bsz, s_len, d)
```
